```python
import math
import jax, jax.numpy as jnp
from jax import lax
import numpy as np

D_MODEL = 1024
BATCH = 8
SEQ = 4096
DEPTH = 1

C_CONV = D_MODEL
K_CONV = 31
HEAD_DIM = 64
N_HEADS = D_MODEL // HEAD_DIM
N_KV_HEADS = N_HEADS // 8
GROUP = N_HEADS // N_KV_HEADS
WINDOW = 128
Q_BLOCK = 128
N_EXPERTS = 32
TOP_K = 4
D_FF = D_MODEL
SWIGLU_LIMIT = 7.0
SWIGLU_ALPHA = 1.702
MOE_BLOCK = 128
EPS = 1e-5

N_GLU = 2 * C_CONV
N_Q = N_HEADS * HEAD_DIM
N_KV = N_KV_HEADS * HEAD_DIM
N_GATE = 2 * D_MODEL
N_IN = N_GLU + N_Q + 2 * N_KV + N_GATE

kernel_name = "hybrid_conformer_swa_sink_alibi_moe"


def rms_norm(x, g):
    xf = x.astype(jnp.float32)
    y = xf * lax.rsqrt(jnp.mean(xf * xf, axis=-1, keepdims=True) + EPS)
    return (y * g.astype(jnp.float32)).astype(x.dtype)


def layer_norm(x, g, b):
    xf = x.astype(jnp.float32)
    mu = jnp.mean(xf, axis=-1, keepdims=True)
    var = jnp.mean(jnp.square(xf - mu), axis=-1, keepdims=True)
    y = (xf - mu) * lax.rsqrt(var + EPS)
    return (y * g.astype(jnp.float32) + b.astype(jnp.float32)).astype(x.dtype)


def alibi_slopes(n_heads):
    return jnp.asarray([2.0 ** (-8.0 * (i + 1) / n_heads) for i in range(n_heads)], dtype=jnp.float32)


def conformer_conv(a, b, conv_w, conv_b, ln_g, ln_b, w_o, b_o):
    u = a * jax.nn.sigmoid(b)
    u = lax.conv_general_dilated(
        u, conv_w[:, None, :].astype(u.dtype), window_strides=(1,),
        padding=[(K_CONV - 1, 0)], dimension_numbers=("NWC", "WIO", "NWC"),
        feature_group_count=C_CONV) + conv_b
    u = jax.nn.silu(layer_norm(u, ln_g, ln_b))
    return u @ w_o + b_o


def sliding_window_attention(q, k, v, sinks):
    B, T = q.shape[0], q.shape[1]
    nb = T // Q_BLOCK
    qb = q.reshape(B, nb, Q_BLOCK, N_KV_HEADS, GROUP, HEAD_DIM)

    def band(t):
        cur = t.reshape(B, nb, Q_BLOCK, N_KV_HEADS, HEAD_DIM)
        prev = jnp.pad(t, ((0, 0), (Q_BLOCK, 0), (0, 0), (0, 0)))[:, :T]
        prev = prev.reshape(B, nb, Q_BLOCK, N_KV_HEADS, HEAD_DIM)
        return jnp.concatenate([prev, cur], axis=2)

    kb, vb = band(k), band(v)
    scale = 1.0 / math.sqrt(HEAD_DIM)
    s = jnp.einsum("bnqkgd,bnskd->bnkgqs", qb, kb).astype(jnp.float32) * scale

    qi = jnp.arange(Q_BLOCK)[:, None]
    sj = jnp.arange(2 * Q_BLOCK)[None, :]
    dist = (qi + Q_BLOCK - sj).astype(jnp.float32)
    key_pos = jnp.arange(nb)[:, None] * Q_BLOCK - Q_BLOCK + jnp.arange(2 * Q_BLOCK)[None, :]
    valid = ((dist >= 0) & (dist < WINDOW))[None] & (key_pos >= 0)[:, None, :]

    slopes = alibi_slopes(N_HEADS).reshape(N_KV_HEADS, GROUP)
    s = s - slopes[:, :, None, None] * dist
    s = jnp.where(valid[None, :, None, None], s, -jnp.inf)

    sink = sinks.astype(jnp.float32).reshape(N_KV_HEADS, GROUP)[None, None, :, :, None, None]
    m = jnp.maximum(jnp.max(s, axis=-1, keepdims=True), sink)
    p = jnp.exp(s - m)
    p = p / (jnp.sum(p, axis=-1, keepdims=True) + jnp.exp(sink - m))
    o = jnp.einsum("bnkgqs,bnskd->bnqkgd", p.astype(v.dtype), vb)
    return o.reshape(B, T, N_HEADS * HEAD_DIM)


def moe_ffn(h, w_router, b_router, w_gate, b_gate, w_up, b_up, w_down, b_down):
    B, T, D = h.shape
    N = B * T
    hf = h.reshape(N, D)
    logits = (hf @ w_router + b_router).astype(jnp.float32)
    top_v, top_e = lax.top_k(logits, TOP_K)
    top_w = jax.nn.softmax(top_v, axis=-1)

    A = N * TOP_K
    flat_e = top_e.reshape(A)
    flat_tok = jnp.repeat(jnp.arange(N, dtype=jnp.int32), TOP_K)
    flat_w = top_w.reshape(A)
    order = jnp.argsort(flat_e, stable=True)
    sorted_e = flat_e[order]
    counts = jnp.bincount(flat_e, length=N_EXPERTS)
    padded = ((counts + MOE_BLOCK - 1) // MOE_BLOCK) * MOE_BLOCK
    pad_end = jnp.cumsum(padded)
    pad_start = pad_end - padded
    start = jnp.cumsum(counts) - counts
    dest = pad_start[sorted_e] + jnp.arange(A) - start[sorted_e]

    n_blocks = (A + N_EXPERTS * (MOE_BLOCK - 1) + MOE_BLOCK - 1) // MOE_BLOCK
    P = n_blocks * MOE_BLOCK
    row_tok = jnp.zeros((P,), jnp.int32).at[dest].set(flat_tok[order])
    row_w = jnp.zeros((P,), jnp.float32).at[dest].set(flat_w[order])
    block_start = jnp.arange(n_blocks) * MOE_BLOCK
    block_e = jnp.minimum(jnp.sum(block_start[:, None] >= pad_end[None, :], axis=1), N_EXPERTS - 1)
    xb = hf[row_tok].reshape(n_blocks, MOE_BLOCK, D)

    def expert_block(args):
        xblk, e = args
        g = xblk @ w_gate[e] + b_gate[e]
        u = xblk @ w_up[e] + b_up[e]
        g = jnp.minimum(g, SWIGLU_LIMIT)
        u = jnp.clip(u, -SWIGLU_LIMIT, SWIGLU_LIMIT)
        a = (u + 1.0) * (g * jax.nn.sigmoid(SWIGLU_ALPHA * g))
        return a @ w_down[e] + b_down[e]

    yb = lax.map(expert_block, (xb, block_e)).reshape(P, D)
    y = jnp.zeros((N, D), jnp.float32).at[row_tok].add(yb.astype(jnp.float32) * row_w[:, None])
    return y.astype(h.dtype).reshape(B, T, D)


def setup_inputs(seed: int = 0) -> dict:
    key = jax.random.key(seed)
    ks = jax.random.split(key, 26)
    f = jnp.float32
    nrm = lambda k, shape, s: jax.random.normal(k, shape, f) * s
    D, F, E = D_MODEL, D_FF, N_EXPERTS
    return {
        "x": nrm(ks[0], (BATCH, SEQ, D), 1.0),
        "g_mix": 1.0 + nrm(ks[1], (D,), 0.05),
        "w_in": nrm(ks[2], (D, N_IN), D ** -0.5),
        "b_in": nrm(ks[3], (N_IN,), 0.02),
        "conv_w": nrm(ks[4], (K_CONV, C_CONV), K_CONV ** -0.5),
        "conv_b": nrm(ks[5], (C_CONV,), 0.02),
        "conv_ln_g": 1.0 + nrm(ks[6], (C_CONV,), 0.05),
        "conv_ln_b": nrm(ks[7], (C_CONV,), 0.02),
        "w_conv_out": nrm(ks[8], (C_CONV, D), C_CONV ** -0.5),
        "b_conv_out": nrm(ks[9], (D,), 0.02),
        "attn_sinks": nrm(ks[10], (N_HEADS,), 0.5),
        "w_attn_out": nrm(ks[11], (N_Q, D), N_Q ** -0.5),
        "b_attn_out": nrm(ks[12], (D,), 0.02),
        "w_out": nrm(ks[13], (D, D), D ** -0.5),
        "g_ffn": 1.0 + nrm(ks[14], (D,), 0.05),
        "w_router": nrm(ks[15], (D, E), D ** -0.5),
        "b_router": nrm(ks[16], (E,), 0.01),
        "w_gate": nrm(ks[17], (E, D, F), D ** -0.5),
        "b_gate": nrm(ks[18], (E, F), 0.02),
        "w_up": nrm(ks[19], (E, D, F), D ** -0.5),
        "b_up": nrm(ks[20], (E, F), 0.02),
        "w_down": nrm(ks[21], (E, F, D), F ** -0.5),
        "b_down": nrm(ks[22], (E, D), 0.02),
        "g_final": 1.0 + nrm(ks[23], (D,), 0.05),
    }


def reference(x, g_mix, w_in, b_in, conv_w, conv_b, conv_ln_g, conv_ln_b, w_conv_out,
              b_conv_out, attn_sinks, w_attn_out, b_attn_out, w_out, g_ffn, w_router,
              b_router, w_gate, b_gate, w_up, b_up, w_down, b_down, g_final):
    B, T, D = x.shape
    for _ in range(DEPTH):
        h = rms_norm(x, g_mix)
        z = h @ w_in + b_in
        o0 = 0
        glu_a = z[..., o0:o0 + C_CONV]; o0 += C_CONV
        glu_b = z[..., o0:o0 + C_CONV]; o0 += C_CONV
        q = z[..., o0:o0 + N_Q].reshape(B, T, N_HEADS, HEAD_DIM); o0 += N_Q
        k = z[..., o0:o0 + N_KV].reshape(B, T, N_KV_HEADS, HEAD_DIM); o0 += N_KV
        v = z[..., o0:o0 + N_KV].reshape(B, T, N_KV_HEADS, HEAD_DIM); o0 += N_KV
        gate_conv = jax.nn.sigmoid(z[..., o0:o0 + D]); o0 += D
        gate_attn = jax.nn.sigmoid(z[..., o0:o0 + D])

        y_conv = conformer_conv(glu_a, glu_b, conv_w, conv_b, conv_ln_g, conv_ln_b,
                                w_conv_out, b_conv_out)
        y_attn = sliding_window_attention(q, k, v, attn_sinks) @ w_attn_out + b_attn_out
        x = x + (gate_conv * y_conv + gate_attn * y_attn) @ w_out

        x = x + moe_ffn(rms_norm(x, g_ffn), w_router, b_router, w_gate, b_gate,
                        w_up, b_up, w_down, b_down)
    return rms_norm(x, g_final)
```

```python
import functools
import math

import jax
import jax.numpy as jnp
from jax import lax
from jax.experimental import pallas as pl
from jax.experimental.pallas import tpu as pltpu

EPS = 1e-5
K_CONV = 31
HEAD_DIM = 64
KV_GROUP = 8
WINDOW = 128
TOP_K = 4
SWIGLU_LIMIT = 7.0
SWIGLU_ALPHA = 1.702

LANES = 128
SUBLANES = 8
HALO = 32
VMEM_LIMIT = 56 * 1024 * 1024

F32 = jnp.float32
BF16 = jnp.bfloat16


def _const_spec(shape):
    nd = len(shape)
    return pl.BlockSpec(shape, lambda *_: (0,) * nd, pipeline_mode=pl.Buffered(1))


def _inproj_kernel(x_ref, g_ref, w_ref, b_ref, u_ref, q_ref, kv_ref, gate_ref, *, c_conv, n_q, n_kv, d_model):
    x = x_ref[...]
    ms = jnp.mean(x * x, axis=-1, keepdims=True)
    h = (x * lax.rsqrt(ms + EPS) * g_ref[...]).astype(BF16)
    cw = 512

    def seg(lo, width):
        return jnp.dot(h, w_ref[:, lo:lo + width], preferred_element_type=F32) + b_ref[:, lo:lo + width]

    for c0 in range(0, c_conv, cw):
        a = seg(c0, cw)
        b = seg(c_conv + c0, cw)
        u_ref[:, c0:c0 + cw] = (a * jax.nn.sigmoid(b)).astype(BF16)
    off = 2 * c_conv
    scale = 1.0 / math.sqrt(HEAD_DIM)
    for c0 in range(0, n_q, cw):
        q_ref[:, c0:c0 + cw] = (seg(off + c0, cw) * scale).astype(BF16)
    off += n_q
    kv_ref[...] = seg(off, 2 * n_kv).astype(BF16)
    off += 2 * n_kv
    for c0 in range(0, 2 * d_model, cw):
        gate_ref[:, c0:c0 + cw] = jax.nn.sigmoid(seg(off + c0, cw)).astype(BF16)


def _inproj(x2, g_mix, w_in, b_in, *, c_conv, n_q, n_kv, tm):
    n, d = x2.shape
    n_in = w_in.shape[1]
    kern = functools.partial(_inproj_kernel, c_conv=c_conv, n_q=n_q, n_kv=n_kv, d_model=d)
    return pl.pallas_call(
        kern,
        grid=(n // tm,),
        in_specs=[
            pl.BlockSpec((tm, d), lambda i: (i, 0)),
            _const_spec((1, d)),
            _const_spec((d, n_in)),
            _const_spec((1, n_in)),
        ],
        out_specs=[
            pl.BlockSpec((tm, c_conv), lambda i: (i, 0)),
            pl.BlockSpec((tm, n_q), lambda i: (i, 0)),
            pl.BlockSpec((tm, 2 * n_kv), lambda i: (i, 0)),
            pl.BlockSpec((tm, 2 * d), lambda i: (i, 0)),
        ],
        out_shape=[
            jax.ShapeDtypeStruct((n, c_conv), BF16),
            jax.ShapeDtypeStruct((n, n_q), BF16),
            jax.ShapeDtypeStruct((n, 2 * n_kv), BF16),
            jax.ShapeDtypeStruct((n, 2 * d), BF16),
        ],
        compiler_params=pltpu.CompilerParams(dimension_semantics=("arbitrary",), vmem_limit_bytes=VMEM_LIMIT),
        name="inproj",
    )(x2, g_mix.reshape(1, d), w_in.astype(BF16), b_in.reshape(1, n_in))


def _conv_kernel(halo_ref, u_ref, w_ref, cb_ref, lg_ref, lb_ref, act_ref, ext_ref, *, tt, strip):
    i = pl.program_id(1)
    rows = tt + HALO
    halo = jnp.where(i > 0, halo_ref[...].astype(F32), 0.0)
    ext = jnp.concatenate([halo, u_ref[...].astype(F32)], axis=0)
    ext_ref[0] = ext
    for s in range(1, SUBLANES):
        ext_ref[s] = pltpu.roll(ext, rows - s, axis=0)

    def body(r, carry):
        base = pl.multiple_of(r * strip, strip)
        acc = jnp.zeros((strip, ext.shape[1]), F32)
        for j in range(K_CONV):
            a, s = divmod(HALO - (K_CONV - 1) + j, SUBLANES)
            acc = acc + ext_ref[s, pl.ds(base + SUBLANES * a, strip), :] * w_ref[j:j + 1, :]
        v = acc + cb_ref[...]
        mu = jnp.mean(v, axis=-1, keepdims=True)
        dv = v - mu
        var = jnp.mean(dv * dv, axis=-1, keepdims=True)
        y = dv * lax.rsqrt(var + EPS) * lg_ref[...] + lb_ref[...]
        act_ref[pl.ds(base, strip), :] = (y * jax.nn.sigmoid(y)).astype(BF16)
        return carry

    lax.fori_loop(0, tt // strip, body, 0)


def _conv(u3, conv_w, conv_b, ln_g, ln_b, *, tt):
    b, t, c = u3.shape
    hb = tt // HALO
    kern = functools.partial(_conv_kernel, tt=tt, strip=16)
    wpad = jnp.zeros((HALO, c), F32).at[:K_CONV].set(conv_w)
    return pl.pallas_call(
        kern,
        grid=(b, t // tt),
        in_specs=[
            pl.BlockSpec((None, HALO, c), lambda bi, i: (bi, jnp.maximum(i * hb - 1, 0), 0)),
            pl.BlockSpec((None, tt, c), lambda bi, i: (bi, i, 0)),
            _const_spec((HALO, c)),
            _const_spec((1, c)),
            _const_spec((1, c)),
            _const_spec((1, c)),
        ],
        out_specs=pl.BlockSpec((None, tt, c), lambda bi, i: (bi, i, 0)),
        out_shape=jax.ShapeDtypeStruct((b, t, c), BF16),
        scratch_shapes=[pltpu.VMEM((SUBLANES, tt + HALO, c), F32)],
        compiler_params=pltpu.CompilerParams(dimension_semantics=("arbitrary", "arbitrary"),
                                             vmem_limit_bytes=VMEM_LIMIT),
        name="conv",
    )(u3, u3, wpad, conv_b.reshape(1, c), ln_g.reshape(1, c), ln_b.reshape(1, c))


def _attn_kernel(q_ref, kvp_ref, kvc_ref, bias_ref, sink_ref, o_ref):
    i = pl.program_id(1)
    qb = WINDOW
    lane = lax.broadcasted_iota(jnp.int32, (2 * qb, LANES), 1)
    lo_half = lane < HEAD_DIM
    kvp = kvp_ref[...].astype(F32)
    kvc = kvc_ref[...].astype(F32)
    kband = jnp.concatenate([kvp[:, :LANES], kvc[:, :LANES]], axis=0)
    vband = jnp.concatenate([kvp[:, LANES:], kvc[:, LANES:]], axis=0)
    kswap = pltpu.roll(kband, HEAD_DIM, axis=1)
    vswap = pltpu.roll(vband, HEAD_DIM, axis=1)
    col = lax.broadcasted_iota(jnp.int32, (4 * qb, 2 * qb), 1)
    first_block_mask = jnp.where((i == 0) & (col < qb), -jnp.inf, 0.0).astype(F32)

    for h in range(2):
        ksrc_lo, ksrc_hi = (kband, kswap) if h == 0 else (kswap, kband)
        vsrc_lo, vsrc_hi = (vband, vswap) if h == 0 else (vswap, vband)
        kext = jnp.concatenate([jnp.where(lo_half, ksrc_lo, 0.0), jnp.where(lo_half, 0.0, ksrc_hi)],
                               axis=0).astype(BF16)
        vext = jnp.concatenate([jnp.where(lo_half, vsrc_lo, 0.0), jnp.where(lo_half, 0.0, vsrc_hi)],
                               axis=0).astype(BF16)
        q4 = jnp.concatenate([q_ref[:, (4 * h + j) * LANES:(4 * h + j + 1) * LANES] for j in range(4)],
                             axis=0)
        s = lax.dot_general(q4, kext, (((1,), (1,)), ((), ())), preferred_element_type=F32)
        ps = []
        for half in range(2):
            sh = s[:, half * 2 * qb:(half + 1) * 2 * qb] + bias_ref[h, half] + first_block_mask
            sink = sink_ref[h, half]
            m = jnp.maximum(jnp.max(sh, axis=-1, keepdims=True), sink)
            p = jnp.exp(sh - m)
            den = jnp.sum(p, axis=-1, keepdims=True) + jnp.exp(sink - m)
            ps.append((p / den).astype(BF16))
        p4 = jnp.concatenate(ps, axis=1)
        o4 = jnp.dot(p4, vext, preferred_element_type=F32)
        for j in range(4):
            o_ref[:, (4 * h + j) * LANES:(4 * h + j + 1) * LANES] = o4[j * qb:(j + 1) * qb].astype(BF16)


def _attn_tables(attn_sinks, n_heads):
    qb = WINDOW
    slopes = jnp.asarray([2.0 ** (-8.0 * (i + 1) / n_heads) for i in range(n_heads)], dtype=F32)
    qi = jnp.arange(qb)[:, None]
    sj = jnp.arange(2 * qb)[None, :]
    dist = (qi + qb - sj).astype(F32)
    valid = (dist >= 0) & (dist < WINDOW)
    heads = (8 * jnp.arange(2)[:, None, None] + jnp.arange(2)[None, :, None] + 2 * jnp.arange(4)[None, None, :])
    sl = slopes[heads]
    bias = jnp.where(valid[None, None, None], -sl[..., None, None] * dist, -jnp.inf)
    bias = bias.reshape(2, 2, 4 * qb, 2 * qb)
    sk = attn_sinks.astype(F32)[heads]
    sink = jnp.broadcast_to(sk[..., None, None], (2, 2, 4, qb, 1)).reshape(2, 2, 4 * qb, 1)
    return bias, sink


def _attn(q3, kv3, attn_sinks, n_heads):
    b, t, nq = q3.shape
    qb = WINDOW
    bias, sink = _attn_tables(attn_sinks, n_heads)
    return pl.pallas_call(
        _attn_kernel,
        grid=(b, t // qb),
        in_specs=[
            pl.BlockSpec((None, qb, nq), lambda bi, i: (bi, i, 0)),
            pl.BlockSpec((None, qb, kv3.shape[2]), lambda bi, i: (bi, jnp.maximum(i - 1, 0), 0)),
            pl.BlockSpec((None, qb, kv3.shape[2]), lambda bi, i: (bi, i, 0)),
            _const_spec(bias.shape),
            _const_spec(sink.shape),
        ],
        out_specs=pl.BlockSpec((None, qb, nq), lambda bi, i: (bi, i, 0)),
        out_shape=jax.ShapeDtypeStruct((b, t, nq), BF16),
        compiler_params=pltpu.CompilerParams(dimension_semantics=("arbitrary", "arbitrary"),
                                             vmem_limit_bytes=VMEM_LIMIT),
        name="attn",
    )(q3, kv3, kv3, bias, sink)


def _merge_kernel(x_ref, act_ref, o_ref, gate_ref, wco_ref, bco_ref, wao_ref, bao_ref, wout_ref,
                  gffn_ref, wr_ref, br_ref,
                  x1_ref, h2r_ref, ri_ref, rw_ref, cnt_ref,
                  carry_ref, *, tm, d, n_exp, tiles_per_chunk):
    i = pl.program_id(0)
    yc = jnp.dot(act_ref[...], wco_ref[...], preferred_element_type=F32) + bco_ref[...]
    ya = jnp.dot(o_ref[...], wao_ref[...], preferred_element_type=F32) + bao_ref[...]
    m = gate_ref[:, :d].astype(F32) * yc + gate_ref[:, d:].astype(F32) * ya
    x1 = x_ref[...] + jnp.dot(m.astype(BF16), wout_ref[...], preferred_element_type=F32)
    x1_ref[...] = x1
    ms = jnp.mean(x1 * x1, axis=-1, keepdims=True)
    h2 = x1 * lax.rsqrt(ms + EPS) * gffn_ref[...]
    nblk = d // LANES
    for cb in range(nblk):
        h2r_ref[pl.ds(cb, tm, stride=nblk), :] = h2[:, cb * LANES:(cb + 1) * LANES]

    h_hi = h2.astype(BF16)
    h_lo = (h2 - h_hi.astype(F32)).astype(BF16)
    wr = wr_ref[...]
    w_hi = wr.astype(BF16)
    w_lo = (wr - w_hi.astype(F32)).astype(BF16)
    logits = (jnp.dot(h_hi, w_hi, preferred_element_type=F32) + jnp.dot(h_hi, w_lo, preferred_element_type=F32)
              + jnp.dot(h_lo, w_hi, preferred_element_type=F32)) + br_ref[...]

    iota = lax.broadcasted_iota(jnp.int32, (tm, n_exp), 1)
    l = logits
    vals, idxs, sels = [], [], []
    for _ in range(TOP_K):
        mk = jnp.max(l, axis=-1, keepdims=True)
        ik = jnp.min(jnp.where(l == mk, iota, n_exp), axis=-1, keepdims=True)
        sel = iota == ik
        l = jnp.where(sel, -jnp.inf, l)
        vals.append(mk)
        idxs.append(ik)
        sels.append(sel)
    exs = [jnp.exp(v - vals[0]) for v in vals]
    den = exs[0] + exs[1] + exs[2] + exs[3]

    @pl.when(i % tiles_per_chunk == 0)
    def _():
        carry_ref[...] = jnp.zeros_like(carry_ref)

    member = (sels[0] | sels[1] | sels[2] | sels[3])
    mf = jnp.where(member, 1.0, 0.0).astype(F32)
    r_i = lax.broadcasted_iota(jnp.int32, (tm, tm), 0)
    c_i = lax.broadcasted_iota(jnp.int32, (tm, tm), 1)
    tri = jnp.where(c_i < r_i, 1.0, 0.0).astype(BF16)
    before = jnp.dot(tri, mf.astype(BF16), preferred_element_type=F32) + carry_ref[...]
    carry_new = carry_ref[...] + jnp.sum(mf, axis=0, keepdims=True)
    carry_ref[...] = carry_new
    cnt_ref[...] = carry_new.astype(jnp.int32)

    io8 = lax.broadcasted_iota(jnp.int32, (tm, 2 * TOP_K), 1)
    ri = jnp.zeros((tm, 2 * TOP_K), jnp.int32)
    rw = jnp.zeros((tm, TOP_K), F32)
    io4 = lax.broadcasted_iota(jnp.int32, (tm, TOP_K), 1)
    for k in range(TOP_K):
        rank = jnp.sum(jnp.where(sels[k], before, 0.0), axis=-1, keepdims=True).astype(jnp.int32)
        ri = jnp.where(io8 == k, idxs[k], ri)
        ri = jnp.where(io8 == TOP_K + k, rank, ri)
        rw = jnp.where(io4 == k, exs[k] / den, rw)
    ri_ref[...] = ri
    rw_ref[...] = rw


def _merge(x2, act, o, gates, w_conv_out, b_conv_out, w_attn_out, b_attn_out, w_out, g_ffn, w_router, b_router,
           *, tm, chunk):
    n, d = x2.shape
    n_exp = w_router.shape[1]
    tiles_per_chunk = chunk // tm
    n_chunks = n // chunk
    nblk = d // LANES
    kern = functools.partial(_merge_kernel, tm=tm, d=d, n_exp=n_exp, tiles_per_chunk=tiles_per_chunk)
    row = lambda i: (i, 0)
    return pl.pallas_call(
        kern,
        grid=(n // tm,),
        in_specs=[
            pl.BlockSpec((tm, d), row),
            pl.BlockSpec((tm, d), row),
            pl.BlockSpec((tm, d), row),
            pl.BlockSpec((tm, 2 * d), row),
            _const_spec((d, d)), _const_spec((1, d)),
            _const_spec((d, d)), _const_spec((1, d)),
            _const_spec((d, d)),
            _const_spec((1, d)),
            _const_spec((d, n_exp)), _const_spec((1, n_exp)),
        ],
        out_specs=[
            pl.BlockSpec((tm, d), row),
            pl.BlockSpec((tm * nblk, LANES), row),
            pl.BlockSpec((tm, 2 * TOP_K), row),
            pl.BlockSpec((tm, TOP_K), row),
            pl.BlockSpec((None, 1, n_exp), lambda i: (i // tiles_per_chunk, 0, 0)),
        ],
        out_shape=[
            jax.ShapeDtypeStruct((n, d), F32),
            jax.ShapeDtypeStruct((n * nblk, LANES), F32),
            jax.ShapeDtypeStruct((n, 2 * TOP_K), jnp.int32),
            jax.ShapeDtypeStruct((n, TOP_K), F32),
            jax.ShapeDtypeStruct((n_chunks, 1, n_exp), jnp.int32),
        ],
        scratch_shapes=[pltpu.VMEM((1, n_exp), F32)],
        compiler_params=pltpu.CompilerParams(dimension_semantics=("arbitrary",), vmem_limit_bytes=VMEM_LIMIT),
        name="merge",
    )(x2, act, o, gates, w_conv_out.astype(BF16), b_conv_out.reshape(1, d), w_attn_out.astype(BF16),
      b_attn_out.reshape(1, d), w_out.astype(BF16), g_ffn.reshape(1, d), w_router, b_router.reshape(1, n_exp))


def _moe_kernel(cnt_ref, e_ref, r_ref, w_ref, h_ref, wg_ref, bg_ref, wu_ref, bu_ref, wd_ref, bd_ref,
                y_ref, stok_ref, sw_ref, off_ref, g_ref, yt_ref, *, chunk, d, n_exp, bm):
    c = pl.program_id(0)
    e = pl.program_id(1)
    nblk = d // LANES
    n_assign = chunk * TOP_K

    @pl.when((c == 0) & (e == 0))
    def _():
        g_ref[...] = jnp.zeros_like(g_ref)

    @pl.when(e == 0)
    def _():
        y_ref[...] = jnp.zeros_like(y_ref)

        def offs(j, acc):
            off_ref[j] = acc
            return acc + cnt_ref[c, j]

        lax.fori_loop(0, n_exp, offs, 0)

        def inv(a, carry):
            p = off_ref[e_ref[a]] + r_ref[a]
            stok_ref[p] = a // TOP_K
            sw_ref[p] = w_ref[a]
            return carry

        lax.fori_loop(0, n_assign, inv, 0)

    cnt = cnt_ref[c, e]
    base = off_ref[e]

    def sub_block(b, carry):
        row0 = base + b * bm
        nv = jnp.minimum(bm, cnt - b * bm)

        def gather(j, cr):
            tok = stok_ref[row0 + j]
            g_ref[pl.ds(pl.multiple_of(j * nblk, nblk), nblk), :] = \
                h_ref[pl.ds(pl.multiple_of(tok * nblk, nblk), nblk), :]
            return cr

        lax.fori_loop(0, nv, gather, 0)
        x = jnp.concatenate([g_ref[pl.ds(cb, bm, stride=nblk), :].astype(BF16) for cb in range(nblk)], axis=1)
        g = jnp.dot(x, wg_ref[...], preferred_element_type=F32) + bg_ref[...]
        u = jnp.dot(x, wu_ref[...], preferred_element_type=F32) + bu_ref[...]
        g = jnp.minimum(g, SWIGLU_LIMIT)
        u = jnp.clip(u, -SWIGLU_LIMIT, SWIGLU_LIMIT)
        a = (u + 1.0) * (g * jax.nn.sigmoid(SWIGLU_ALPHA * g))
        y = jnp.dot(a.astype(BF16), wd_ref[...], preferred_element_type=F32) + bd_ref[...]
        for cb in range(nblk):
            yt_ref[pl.ds(cb, bm, stride=nblk), :] = y[:, cb * LANES:(cb + 1) * LANES]

        def scatter(j, cr):
            tok = stok_ref[row0 + j]
            dst = pl.ds(pl.multiple_of(tok * nblk, nblk), nblk)
            y_ref[dst, :] = y_ref[dst, :] + sw_ref[row0 + j] * yt_ref[pl.ds(pl.multiple_of(j * nblk, nblk), nblk), :]
            return cr

        lax.fori_loop(0, nv, scatter, 0)
        return carry

    lax.fori_loop(0, (cnt + bm - 1) // bm, sub_block, 0)


def _moe(counts, e_flat, r_flat, w_flat, h2r, wg, bg, wu, bu, wd, bd, *, chunk, d, bm):
    n_chunks, n_exp = counts.shape
    nblk = d // LANES
    f = wg.shape[2]
    n_assign = chunk * TOP_K
    kern = functools.partial(_moe_kernel, chunk=chunk, d=d, n_exp=n_exp, bm=bm)
    smem_blk = lambda: pl.BlockSpec((n_assign,), lambda c, e, cnt: (c,), memory_space=pltpu.SMEM)
    wspec = lambda s: pl.BlockSpec((None,) + s, lambda c, e, cnt: (e, 0, 0))
    grid_spec = pltpu.PrefetchScalarGridSpec(
        num_scalar_prefetch=1,
        grid=(n_chunks, n_exp),
        in_specs=[
            smem_blk(), smem_blk(), smem_blk(),
            pl.BlockSpec((chunk * nblk, LANES), lambda c, e, cnt: (c, 0), pipeline_mode=pl.Buffered(1)),
            wspec((d, f)), wspec((1, f)), wspec((d, f)), wspec((1, f)), wspec((f, d)), wspec((1, d)),
        ],
        out_specs=pl.BlockSpec((chunk * nblk, LANES), lambda c, e, cnt: (c, 0)),
        scratch_shapes=[
            pltpu.SMEM((n_assign,), jnp.int32),
            pltpu.SMEM((n_assign,), F32),
            pltpu.SMEM((n_exp,), jnp.int32),
            pltpu.VMEM((bm * nblk, LANES), F32),
            pltpu.VMEM((bm * nblk, LANES), F32),
        ],
    )
    return pl.pallas_call(
        kern,
        grid_spec=grid_spec,
        out_shape=jax.ShapeDtypeStruct((n_chunks * chunk * nblk, LANES), F32),
        compiler_params=pltpu.CompilerParams(dimension_semantics=("arbitrary", "arbitrary"),
                                             vmem_limit_bytes=VMEM_LIMIT),
        name="moe",
    )(counts, e_flat, r_flat, w_flat, h2r, wg, bg, wu, bu, wd, bd)


def _final_kernel(x1_ref, y_ref, g_ref, out_ref, *, tm, d):
    nblk = d // LANES
    y = jnp.concatenate([y_ref[pl.ds(cb, tm, stride=nblk), :] for cb in range(nblk)], axis=1)
    x = x1_ref[...] + y
    ms = jnp.mean(x * x, axis=-1, keepdims=True)
    out_ref[...] = x * lax.rsqrt(ms + EPS) * g_ref[...]


def _final(x1, yr, g_final, *, tm):
    n, d = x1.shape
    nblk = d // LANES
    return pl.pallas_call(
        functools.partial(_final_kernel, tm=tm, d=d),
        grid=(n // tm,),
        in_specs=[
            pl.BlockSpec((tm, d), lambda i: (i, 0)),
            pl.BlockSpec((tm * nblk, LANES), lambda i: (i, 0)),
            _const_spec((1, d)),
        ],
        out_specs=pl.BlockSpec((tm, d), lambda i: (i, 0)),
        out_shape=jax.ShapeDtypeStruct((n, d), F32),
        compiler_params=pltpu.CompilerParams(dimension_semantics=("arbitrary",), vmem_limit_bytes=VMEM_LIMIT),
        name="final",
    )(x1, yr, g_final.reshape(1, d))


def _block_sizes(batch, seq):
    n = batch * seq
    tm = min(512, n)
    tt = min(256, seq)
    chunk = min(2048, n)
    return tm, tt, chunk


def kernel(x, g_mix, w_in, b_in, conv_w, conv_b, conv_ln_g, conv_ln_b, w_conv_out, b_conv_out, attn_sinks,
           w_attn_out, b_attn_out, w_out, g_ffn, w_router, b_router, w_gate, b_gate, w_up, b_up, w_down,
           b_down, g_final):
    batch, seq, d = x.shape
    n = batch * seq
    c_conv = conv_w.shape[1]
    n_heads = attn_sinks.shape[0]
    n_q = n_heads * HEAD_DIM
    n_kv = (n_heads // KV_GROUP) * HEAD_DIM
    n_exp = w_router.shape[1]
    assert n_kv == LANES and seq % WINDOW == 0 and d % LANES == 0
    tm, tt, chunk = _block_sizes(batch, seq)
    assert n % tm == 0 and seq % tt == 0 and n % chunk == 0 and chunk % tm == 0

    x2 = x.reshape(n, d)
    u, q, kv, gates = _inproj(x2, g_mix, w_in, b_in, c_conv=c_conv, n_q=n_q, n_kv=n_kv, tm=tm)
    act = _conv(u.reshape(batch, seq, c_conv), conv_w, conv_b, conv_ln_g, conv_ln_b, tt=tt)
    o = _attn(q.reshape(batch, seq, n_q), kv.reshape(batch, seq, 2 * n_kv), attn_sinks, n_heads)
    x1, h2r, ri, rw, counts = _merge(
        x2, act.reshape(n, c_conv), o.reshape(n, n_q), gates, w_conv_out, b_conv_out, w_attn_out, b_attn_out,
        w_out, g_ffn, w_router, b_router, tm=tm, chunk=chunk)
    e_flat = ri[:, :TOP_K].reshape(-1)
    r_flat = ri[:, TOP_K:].reshape(-1)
    w_flat = rw.reshape(-1)
    yr = _moe(counts.reshape(-1, n_exp), e_flat, r_flat, w_flat, h2r,
              w_gate.astype(BF16), b_gate.reshape(n_exp, 1, -1), w_up.astype(BF16), b_up.reshape(n_exp, 1, -1),
              w_down.astype(BF16), b_down.reshape(n_exp, 1, -1), chunk=chunk, d=d, bm=128)
    out = _final(x1, yr, g_final, tm=tm)
    return out.reshape(batch, seq, d)
```

```python
import functools
import math

import jax
import jax.numpy as jnp
from jax import lax
from jax.experimental import pallas as pl
from jax.experimental.pallas import tpu as pltpu

EPS = 1e-5
K_CONV = 31
HEAD_DIM = 64
KV_GROUP = 8
WINDOW = 128
TOP_K = 4
SWIGLU_LIMIT = 7.0
SWIGLU_ALPHA = 1.702

LANES = 128
SUBLANES = 8
HALO = 32
VMEM_LIMIT = 56 * 1024 * 1024

F32 = jnp.float32
BF16 = jnp.bfloat16


def _const_spec(shape):
    nd = len(shape)
    return pl.BlockSpec(shape, lambda *_: (0,) * nd, pipeline_mode=pl.Buffered(1))


def _inproj_kernel(x_ref, g_ref, w_ref, b_ref, u_ref, q_ref, kv_ref, gate_ref, *, c_conv, n_q, n_kv, d_model):
    x = x_ref[...]
    ms = jnp.mean(x * x, axis=-1, keepdims=True)
    h = (x * lax.rsqrt(ms + EPS) * g_ref[...]).astype(BF16)
    cw = 512

    def seg(lo, width):
        return jnp.dot(h, w_ref[:, lo:lo + width], preferred_element_type=F32) + b_ref[:, lo:lo + width]

    for c0 in range(0, c_conv, cw):
        a = seg(c0, cw)
        b = seg(c_conv + c0, cw)
        u_ref[:, c0:c0 + cw] = (a * jax.nn.sigmoid(b)).astype(BF16)
    off = 2 * c_conv
    scale = 1.0 / math.sqrt(HEAD_DIM)
    for c0 in range(0, n_q, cw):
        q_ref[:, c0:c0 + cw] = (seg(off + c0, cw) * scale).astype(BF16)
    off += n_q
    kv_ref[...] = seg(off, 2 * n_kv).astype(BF16)
    off += 2 * n_kv
    for c0 in range(0, 2 * d_model, cw):
        gate_ref[:, c0:c0 + cw] = jax.nn.sigmoid(seg(off + c0, cw)).astype(BF16)


def _inproj(x2, g_mix, w_in, b_in, *, c_conv, n_q, n_kv, tm):
    n, d = x2.shape
    n_in = w_in.shape[1]
    kern = functools.partial(_inproj_kernel, c_conv=c_conv, n_q=n_q, n_kv=n_kv, d_model=d)
    return pl.pallas_call(
        kern,
        grid=(n // tm,),
        in_specs=[
            pl.BlockSpec((tm, d), lambda i: (i, 0)),
            _const_spec((1, d)),
            _const_spec((d, n_in)),
            _const_spec((1, n_in)),
        ],
        out_specs=[
            pl.BlockSpec((tm, c_conv), lambda i: (i, 0)),
            pl.BlockSpec((tm, n_q), lambda i: (i, 0)),
            pl.BlockSpec((tm, 2 * n_kv), lambda i: (i, 0)),
            pl.BlockSpec((tm, 2 * d), lambda i: (i, 0)),
        ],
        out_shape=[
            jax.ShapeDtypeStruct((n, c_conv), BF16),
            jax.ShapeDtypeStruct((n, n_q), BF16),
            jax.ShapeDtypeStruct((n, 2 * n_kv), BF16),
            jax.ShapeDtypeStruct((n, 2 * d), BF16),
        ],
        compiler_params=pltpu.CompilerParams(dimension_semantics=("arbitrary",), vmem_limit_bytes=VMEM_LIMIT),
        name="inproj",
    )(x2, g_mix.reshape(1, d), w_in.astype(BF16), b_in.reshape(1, n_in))


def _conv_kernel(halo_ref, u_ref, w_ref, cb_ref, lg_ref, lb_ref, act_ref, ext_ref, conv_ref, *, tt, strip):
    i = pl.program_id(1)
    rows = tt + HALO
    halo = jnp.where(i > 0, halo_ref[...].astype(F32), 0.0)
    ext = jnp.concatenate([halo, u_ref[...].astype(F32)], axis=0)
    ext_ref[0] = ext
    for s in range(1, SUBLANES):
        ext_ref[s] = pltpu.roll(ext, rows - s, axis=0)

    rgrp = 8
    rchunk = rgrp * SUBLANES
    for cb in range(ext.shape[1] // LANES):
        lanes = slice(cb * LANES, (cb + 1) * LANES)

        def taps(rc, carry, lanes=lanes):
            base = pl.multiple_of(rc * rchunk, rchunk)
            accs = [jnp.zeros((SUBLANES, LANES), F32) for _ in range(rgrp)]
            for j in range(K_CONV):
                a, s = divmod(HALO - (K_CONV - 1) + j, SUBLANES)
                wv = w_ref[j, :, lanes]
                for g in range(rgrp):
                    accs[g] = accs[g] + ext_ref[s, pl.ds(base + SUBLANES * (a + g), SUBLANES), lanes] * wv
            for g in range(rgrp):
                conv_ref[pl.ds(base + SUBLANES * g, SUBLANES), lanes] = accs[g]
            return carry

        lax.fori_loop(0, tt // rchunk, taps, 0)

    for r in range(tt // strip):
        rows_r = slice(r * strip, (r + 1) * strip)
        v = conv_ref[rows_r, :] + cb_ref[...]
        mu = jnp.mean(v, axis=-1, keepdims=True)
        dv = v - mu
        var = jnp.mean(dv * dv, axis=-1, keepdims=True)
        y = dv * lax.rsqrt(var + EPS) * lg_ref[...] + lb_ref[...]
        act_ref[rows_r, :] = (y * jax.nn.sigmoid(y)).astype(BF16)


def _conv(u3, conv_w, conv_b, ln_g, ln_b, *, tt):
    b, t, c = u3.shape
    hb = tt // HALO
    kern = functools.partial(_conv_kernel, tt=tt, strip=32)
    wpad = jnp.broadcast_to(conv_w[:, None, :], (K_CONV, SUBLANES, c))
    return pl.pallas_call(
        kern,
        grid=(b, t // tt),
        in_specs=[
            pl.BlockSpec((None, HALO, c), lambda bi, i: (bi, jnp.maximum(i * hb - 1, 0), 0)),
            pl.BlockSpec((None, tt, c), lambda bi, i: (bi, i, 0)),
            _const_spec((K_CONV, SUBLANES, c)),
            _const_spec((1, c)),
            _const_spec((1, c)),
            _const_spec((1, c)),
        ],
        out_specs=pl.BlockSpec((None, tt, c), lambda bi, i: (bi, i, 0)),
        out_shape=jax.ShapeDtypeStruct((b, t, c), BF16),
        scratch_shapes=[pltpu.VMEM((SUBLANES, tt + HALO, c), F32), pltpu.VMEM((tt, c), F32)],
        compiler_params=pltpu.CompilerParams(dimension_semantics=("arbitrary", "arbitrary"),
                                             vmem_limit_bytes=VMEM_LIMIT),
        name="conv",
    )(u3, u3, wpad, conv_b.reshape(1, c), ln_g.reshape(1, c), ln_b.reshape(1, c))


def _attn_kernel(q_ref, kvp_ref, kvc_ref, bias_ref, sink_ref, o_ref):
    i = pl.program_id(1)
    qb = WINDOW
    lane = lax.broadcasted_iota(jnp.int32, (2 * qb, LANES), 1)
    lo_half = lane < HEAD_DIM
    kvp = kvp_ref[...].astype(F32)
    kvc = kvc_ref[...].astype(F32)
    kband = jnp.concatenate([kvp[:, :LANES], kvc[:, :LANES]], axis=0)
    vband = jnp.concatenate([kvp[:, LANES:], kvc[:, LANES:]], axis=0)
    kswap = pltpu.roll(kband, HEAD_DIM, axis=1)
    vswap = pltpu.roll(vband, HEAD_DIM, axis=1)
    col = lax.broadcasted_iota(jnp.int32, (qb, 2 * qb), 1)
    first_block_mask = jnp.where((i == 0) & (col < qb), -jnp.inf, 0.0).astype(F32)
    ind_r = lax.broadcasted_iota(jnp.int32, (4 * qb, LANES), 0) < 2 * qb
    ind_c = lax.broadcasted_iota(jnp.int32, (4 * qb, LANES), 1) < HEAD_DIM
    ind = jnp.where(ind_r == ind_c, 1.0, 0.0).astype(BF16)
    lo_lane = lax.broadcasted_iota(jnp.int32, (qb, LANES), 1) < HEAD_DIM

    for h in range(2):
        ksrc_lo, ksrc_hi = (kband, kswap) if h == 0 else (kswap, kband)
        vsrc_lo, vsrc_hi = (vband, vswap) if h == 0 else (vswap, vband)
        kext = jnp.concatenate([jnp.where(lo_half, ksrc_lo, 0.0), jnp.where(lo_half, 0.0, ksrc_hi)],
                               axis=0).astype(BF16)
        vext = jnp.concatenate([jnp.where(lo_half, vsrc_lo, 0.0), jnp.where(lo_half, 0.0, vsrc_hi)],
                               axis=0).astype(BF16)
        vext2 = jnp.concatenate([vext, ind], axis=1)
        for j in range(4):
            cols = slice((4 * h + j) * LANES, (4 * h + j + 1) * LANES)
            s = lax.dot_general(q_ref[:, cols], kext, (((1,), (1,)), ((), ())),
                                preferred_element_type=F32)
            ps, es = [], []
            for half in range(2):
                sh = s[:, half * 2 * qb:(half + 1) * 2 * qb] + bias_ref[h, half, j] + first_block_mask
                sink = sink_ref[8 * h + 2 * j + half]
                m = jnp.maximum(jnp.max(sh, axis=-1, keepdims=True), sink)
                ps.append(jnp.exp(sh - m).astype(BF16))
                es.append(jnp.exp(sink - m))
            r = jnp.dot(jnp.concatenate(ps, axis=1), vext2, preferred_element_type=F32)
            den = r[:, LANES:] + jnp.where(lo_lane, es[0], es[1])
            o_ref[:, cols] = (r[:, :LANES] / den).astype(BF16)


def _alibi_bias(n_heads):
    qb = WINDOW
    slopes = jnp.asarray([2.0 ** (-8.0 * (i + 1) / n_heads) for i in range(n_heads)], dtype=F32)
    qi = jnp.arange(qb)[:, None]
    sj = jnp.arange(2 * qb)[None, :]
    dist = (qi + qb - sj).astype(F32)
    valid = (dist >= 0) & (dist < WINDOW)
    heads = (8 * jnp.arange(2)[:, None, None] + jnp.arange(2)[None, :, None] + 2 * jnp.arange(4)[None, None, :])
    sl = slopes[heads]
    return jnp.where(valid[None, None, None], -sl[..., None, None] * dist, -jnp.inf)


def _attn(q3, kv3, attn_sinks, n_heads):
    b, t, nq = q3.shape
    qb = WINDOW
    bias = _alibi_bias(n_heads)
    return pl.pallas_call(
        _attn_kernel,
        grid=(b, t // qb),
        in_specs=[
            pl.BlockSpec((None, qb, nq), lambda bi, i: (bi, i, 0)),
            pl.BlockSpec((None, qb, kv3.shape[2]), lambda bi, i: (bi, jnp.maximum(i - 1, 0), 0)),
            pl.BlockSpec((None, qb, kv3.shape[2]), lambda bi, i: (bi, i, 0)),
            _const_spec(bias.shape),
            pl.BlockSpec(memory_space=pltpu.SMEM),
        ],
        out_specs=pl.BlockSpec((None, qb, nq), lambda bi, i: (bi, i, 0)),
        out_shape=jax.ShapeDtypeStruct((b, t, nq), BF16),
        compiler_params=pltpu.CompilerParams(dimension_semantics=("arbitrary", "arbitrary"),
                                             vmem_limit_bytes=VMEM_LIMIT),
        name="attn",
    )(q3, kv3, kv3, bias, attn_sinks.astype(F32))


def _merge_kernel(x_ref, act_ref, o_ref, gate_ref, wco_ref, bco_ref, wao_ref, bao_ref, wout_ref,
                  gffn_ref, wr_ref, br_ref,
                  x1_ref, hp_ref, ri_ref, rw_ref, cnt_ref,
                  carry_ref, *, tm, d, n_exp, tiles_per_chunk):
    i = pl.program_id(0)
    yc = jnp.dot(act_ref[...], wco_ref[...], preferred_element_type=F32) + bco_ref[...]
    ya = jnp.dot(o_ref[...], wao_ref[...], preferred_element_type=F32) + bao_ref[...]
    m = gate_ref[:, :d].astype(F32) * yc + gate_ref[:, d:].astype(F32) * ya
    x1 = x_ref[...] + jnp.dot(m.astype(BF16), wout_ref[...], preferred_element_type=F32)
    x1_ref[...] = x1
    ms = jnp.mean(x1 * x1, axis=-1, keepdims=True)
    h2 = x1 * lax.rsqrt(ms + EPS) * gffn_ref[...]
    pk = d // (2 * LANES)
    for s in range(pk):
        hp_ref[pl.ds(s, tm, stride=pk), :] = _pack_pair(h2[:, s * LANES:(s + 1) * LANES],
                                                        h2[:, d // 2 + s * LANES:d // 2 + (s + 1) * LANES])

    h_hi = h2.astype(BF16)
    h_lo = (h2 - h_hi.astype(F32)).astype(BF16)
    wr = wr_ref[...]
    w_hi = wr.astype(BF16)
    w_lo = (wr - w_hi.astype(F32)).astype(BF16)
    logits = (jnp.dot(h_hi, w_hi, preferred_element_type=F32) + jnp.dot(h_hi, w_lo, preferred_element_type=F32)
              + jnp.dot(h_lo, w_hi, preferred_element_type=F32)) + br_ref[...]

    iota = lax.broadcasted_iota(jnp.int32, (tm, n_exp), 1)
    l = logits
    vals, idxs, sels = [], [], []
    for _ in range(TOP_K):
        mk = jnp.max(l, axis=-1, keepdims=True)
        ik = jnp.min(jnp.where(l == mk, iota, n_exp), axis=-1, keepdims=True)
        sel = iota == ik
        l = jnp.where(sel, -jnp.inf, l)
        vals.append(mk)
        idxs.append(ik)
        sels.append(sel)
    exs = [jnp.exp(v - vals[0]) for v in vals]
    den = exs[0] + exs[1] + exs[2] + exs[3]

    @pl.when(i % tiles_per_chunk == 0)
    def _():
        carry_ref[...] = jnp.zeros_like(carry_ref)

    member = (sels[0] | sels[1] | sels[2] | sels[3])
    mf = jnp.where(member, 1.0, 0.0).astype(F32)
    r_i = lax.broadcasted_iota(jnp.int32, (tm, tm), 0)
    c_i = lax.broadcasted_iota(jnp.int32, (tm, tm), 1)
    tri = jnp.where(c_i < r_i, 1.0, 0.0).astype(BF16)
    before = jnp.dot(tri, mf.astype(BF16), preferred_element_type=F32) + carry_ref[...]
    carry_new = carry_ref[...] + jnp.sum(mf, axis=0, keepdims=True)
    carry_ref[...] = carry_new
    cnt_ref[...] = carry_new.astype(jnp.int32)

    io8 = lax.broadcasted_iota(jnp.int32, (tm, 2 * TOP_K), 1)
    ri = jnp.zeros((tm, 2 * TOP_K), jnp.int32)
    rw = jnp.zeros((tm, TOP_K), F32)
    io4 = lax.broadcasted_iota(jnp.int32, (tm, TOP_K), 1)
    for k in range(TOP_K):
        rank = jnp.sum(jnp.where(sels[k], before, 0.0), axis=-1, keepdims=True).astype(jnp.int32)
        ri = jnp.where(io8 == k, idxs[k], ri)
        ri = jnp.where(io8 == TOP_K + k, rank, ri)
        rw = jnp.where(io4 == k, exs[k] / den, rw)
    ri_ref[...] = ri
    rw_ref[...] = rw


def _merge(x2, act, o, gates, w_conv_out, b_conv_out, w_attn_out, b_attn_out, w_out, g_ffn, w_router, b_router,
           *, tm, chunk):
    n, d = x2.shape
    n_exp = w_router.shape[1]
    tiles_per_chunk = chunk // tm
    n_chunks = n // chunk
    pk = d // (2 * LANES)
    kern = functools.partial(_merge_kernel, tm=tm, d=d, n_exp=n_exp, tiles_per_chunk=tiles_per_chunk)
    row = lambda i: (i, 0)
    return pl.pallas_call(
        kern,
        grid=(n // tm,),
        in_specs=[
            pl.BlockSpec((tm, d), row),
            pl.BlockSpec((tm, d), row),
            pl.BlockSpec((tm, d), row),
            pl.BlockSpec((tm, 2 * d), row),
            _const_spec((d, d)), _const_spec((1, d)),
            _const_spec((d, d)), _const_spec((1, d)),
            _const_spec((d, d)),
            _const_spec((1, d)),
            _const_spec((d, n_exp)), _const_spec((1, n_exp)),
        ],
        out_specs=[
            pl.BlockSpec((tm, d), row),
            pl.BlockSpec((tm * pk, LANES), row),
            pl.BlockSpec((tm, 2 * TOP_K), row),
            pl.BlockSpec((tm, TOP_K), row),
            pl.BlockSpec((None, 1, n_exp), lambda i: (i // tiles_per_chunk, 0, 0)),
        ],
        out_shape=[
            jax.ShapeDtypeStruct((n, d), F32),
            jax.ShapeDtypeStruct((n * pk, LANES), jnp.int32),
            jax.ShapeDtypeStruct((n, 2 * TOP_K), jnp.int32),
            jax.ShapeDtypeStruct((n, TOP_K), F32),
            jax.ShapeDtypeStruct((n_chunks, 1, n_exp), jnp.int32),
        ],
        scratch_shapes=[pltpu.VMEM((1, n_exp), F32)],
        compiler_params=pltpu.CompilerParams(dimension_semantics=("arbitrary",), vmem_limit_bytes=VMEM_LIMIT),
        name="merge",
    )(x2, act, o, gates, w_conv_out.astype(BF16), b_conv_out.reshape(1, d), w_attn_out.astype(BF16),
      b_attn_out.reshape(1, d), w_out.astype(BF16), g_ffn.reshape(1, d), w_router, b_router.reshape(1, n_exp))


def _unpack_pair(word):
    lo = lax.bitcast_convert_type(lax.shift_left(word, 16), F32)
    hi = lax.bitcast_convert_type(word & jnp.int32(-65536), F32)
    return lo, hi


def _pack_pair(lo, hi):
    lo_b = lax.bitcast_convert_type(lo.astype(BF16).astype(F32), jnp.int32)
    hi_b = lax.bitcast_convert_type(hi.astype(BF16).astype(F32), jnp.int32)
    return lax.shift_right_logical(lo_b, 16) | hi_b


def _moe_kernel(cnt_ref, e_ref, r_ref, w_ref, hp_ref, wg_ref, bg_ref, wu_ref, bu_ref, wd_ref, bd_ref,
                y_ref, xs_ref, off_ref, pos_ref, *, chunk, d, n_exp, bm):
    c = pl.program_id(0)
    e = pl.program_id(1)
    pk = d // (2 * LANES)
    half = d // 2

    @pl.when((c == 0) & (e == 0))
    def _():
        xs_ref[...] = jnp.zeros_like(xs_ref)

    @pl.when(e == 0)
    def _():
        def offs(j, acc):
            off_ref[j] = acc
            return acc + cnt_ref[c, j]

        lax.fori_loop(0, n_exp, offs, 0)

        def dispatch(t, carry):
            row = hp_ref[pl.ds(pl.multiple_of(t * pk, pk), pk), :]
            for k in range(TOP_K):
                a = t * TOP_K + k
                p = off_ref[e_ref[a]] + r_ref[a]
                pos_ref[a] = p
                xs_ref[pl.ds(pl.multiple_of(p * pk, pk), pk), :] = row
            return carry

        lax.fori_loop(0, chunk, dispatch, 0, unroll=4)

    cnt = cnt_ref[c, e]
    base = off_ref[e]
    row_id = lax.broadcasted_iota(jnp.int32, (bm, LANES), 0)

    def block(b, carry):
        row0 = (base + b * bm) * pk
        nv = cnt - b * bm
        words = [xs_ref[pl.ds(row0 + s, bm, stride=pk), :] for s in range(pk)]
        halves = [_unpack_pair(w) for w in words]
        x = jnp.concatenate([h[0].astype(BF16) for h in halves] + [h[1].astype(BF16) for h in halves], axis=1)
        g = jnp.dot(x, wg_ref[...], preferred_element_type=F32) + bg_ref[...]
        u = jnp.dot(x, wu_ref[...], preferred_element_type=F32) + bu_ref[...]
        g = jnp.minimum(g, SWIGLU_LIMIT)
        u = jnp.clip(u, -SWIGLU_LIMIT, SWIGLU_LIMIT)
        a = (u + 1.0) * (g * jax.nn.sigmoid(SWIGLU_ALPHA * g))
        y = jnp.dot(a.astype(BF16), wd_ref[...], preferred_element_type=F32) + bd_ref[...]
        keep = row_id < nv
        for s in range(pk):
            new = _pack_pair(y[:, s * LANES:(s + 1) * LANES], y[:, half + s * LANES:half + (s + 1) * LANES])
            xs_ref[pl.ds(row0 + s, bm, stride=pk), :] = jnp.where(keep, new, words[s])
        return carry

    lax.fori_loop(0, (cnt + bm - 1) // bm, block, 0)

    @pl.when(e == n_exp - 1)
    def _():
        def combine(t, carry):
            lo = jnp.zeros((pk, LANES), F32)
            hi = jnp.zeros((pk, LANES), F32)
            for k in range(TOP_K):
                a = t * TOP_K + k
                p = pos_ref[a]
                wl, wh = _unpack_pair(xs_ref[pl.ds(pl.multiple_of(p * pk, pk), pk), :])
                lo = lo + w_ref[a] * wl
                hi = hi + w_ref[a] * wh
            y_ref[pl.ds(pl.multiple_of(t * 2 * pk, 2 * pk), pk), :] = lo
            y_ref[pl.ds(pl.multiple_of(t * 2 * pk, 2 * pk) + pk, pk), :] = hi
            return carry

        lax.fori_loop(0, chunk, combine, 0, unroll=4)


def _moe(counts, e_flat, r_flat, w_flat, hp, wg, bg, wu, bu, wd, bd, *, chunk, d, bm):
    n_chunks, n_exp = counts.shape
    nblk = d // LANES
    pk = nblk // 2
    f = wg.shape[2]
    n_assign = chunk * TOP_K
    kern = functools.partial(_moe_kernel, chunk=chunk, d=d, n_exp=n_exp, bm=bm)
    smem_blk = lambda: pl.BlockSpec((n_assign,), lambda c, e, cnt: (c,), memory_space=pltpu.SMEM)
    wspec = lambda s: pl.BlockSpec((None,) + s, lambda c, e, cnt: (e, 0, 0))
    grid_spec = pltpu.PrefetchScalarGridSpec(
        num_scalar_prefetch=1,
        grid=(n_chunks, n_exp),
        in_specs=[
            smem_blk(), smem_blk(), smem_blk(),
            pl.BlockSpec((chunk * pk, LANES), lambda c, e, cnt: (c, 0), pipeline_mode=pl.Buffered(1)),
            wspec((d, f)), wspec((1, f)), wspec((d, f)), wspec((1, f)), wspec((f, d)), wspec((1, d)),
        ],
        out_specs=pl.BlockSpec((chunk * nblk, LANES), lambda c, e, cnt: (c, 0)),
        scratch_shapes=[
            pltpu.VMEM(((n_assign + bm) * pk, LANES), jnp.int32),
            pltpu.SMEM((n_exp,), jnp.int32),
            pltpu.SMEM((n_assign,), jnp.int32),
        ],
    )
    return pl.pallas_call(
        kern,
        grid_spec=grid_spec,
        out_shape=jax.ShapeDtypeStruct((n_chunks * chunk * nblk, LANES), F32),
        compiler_params=pltpu.CompilerParams(dimension_semantics=("arbitrary", "arbitrary"),
                                             vmem_limit_bytes=VMEM_LIMIT),
        name="moe",
    )(counts, e_flat, r_flat, w_flat, hp, wg, bg, wu, bu, wd, bd)


def _final_kernel(x1_ref, y_ref, g_ref, out_ref, *, tm, d):
    nblk = d // LANES
    y = jnp.concatenate([y_ref[pl.ds(cb, tm, stride=nblk), :] for cb in range(nblk)], axis=1)
    x = x1_ref[...] + y
    ms = jnp.mean(x * x, axis=-1, keepdims=True)
    out_ref[...] = x * lax.rsqrt(ms + EPS) * g_ref[...]


def _final(x1, yr, g_final, *, tm):
    n, d = x1.shape
    nblk = d // LANES
    return pl.pallas_call(
        functools.partial(_final_kernel, tm=tm, d=d),
        grid=(n // tm,),
        in_specs=[
            pl.BlockSpec((tm, d), lambda i: (i, 0)),
            pl.BlockSpec((tm * nblk, LANES), lambda i: (i, 0)),
            _const_spec((1, d)),
        ],
        out_specs=pl.BlockSpec((tm, d), lambda i: (i, 0)),
        out_shape=jax.ShapeDtypeStruct((n, d), F32),
        compiler_params=pltpu.CompilerParams(dimension_semantics=("arbitrary",), vmem_limit_bytes=VMEM_LIMIT),
        name="final",
    )(x1, yr, g_final.reshape(1, d))


def _block_sizes(batch, seq):
    n = batch * seq
    tm = min(512, n)
    tt = min(256, seq)
    chunk = min(2048, n)
    return tm, tt, chunk


def kernel(x, g_mix, w_in, b_in, conv_w, conv_b, conv_ln_g, conv_ln_b, w_conv_out, b_conv_out, attn_sinks,
           w_attn_out, b_attn_out, w_out, g_ffn, w_router, b_router, w_gate, b_gate, w_up, b_up, w_down,
           b_down, g_final):
    batch, seq, d = x.shape
    n = batch * seq
    c_conv = conv_w.shape[1]
    n_heads = attn_sinks.shape[0]
    n_q = n_heads * HEAD_DIM
    n_kv = (n_heads // KV_GROUP) * HEAD_DIM
    n_exp = w_router.shape[1]
    assert n_kv == LANES and seq % WINDOW == 0 and d % LANES == 0
    tm, tt, chunk = _block_sizes(batch, seq)
    assert n % tm == 0 and seq % tt == 0 and n % chunk == 0 and chunk % tm == 0

    x2 = x.reshape(n, d)
    u, q, kv, gates = _inproj(x2, g_mix, w_in, b_in, c_conv=c_conv, n_q=n_q, n_kv=n_kv, tm=tm)
    act = _conv(u.reshape(batch, seq, c_conv), conv_w, conv_b, conv_ln_g, conv_ln_b, tt=tt)
    o = _attn(q.reshape(batch, seq, n_q), kv.reshape(batch, seq, 2 * n_kv), attn_sinks, n_heads)
    x1, hp, ri, rw, counts = _merge(
        x2, act.reshape(n, c_conv), o.reshape(n, n_q), gates, w_conv_out, b_conv_out, w_attn_out, b_attn_out,
        w_out, g_ffn, w_router, b_router, tm=tm, chunk=chunk)
    e_flat = ri[:, :TOP_K].reshape(-1)
    r_flat = ri[:, TOP_K:].reshape(-1)
    w_flat = rw.reshape(-1)
    yr = _moe(counts.reshape(-1, n_exp), e_flat, r_flat, w_flat, hp,
              w_gate.astype(BF16), b_gate.reshape(n_exp, 1, -1), w_up.astype(BF16), b_up.reshape(n_exp, 1, -1),
              w_down.astype(BF16), b_down.reshape(n_exp, 1, -1), chunk=chunk, d=d, bm=128)
    out = _final(x1, yr, g_final, tm=tm)
    return out.reshape(batch, seq, d)
```

```python
import functools
import math

import jax
import jax.numpy as jnp
from jax import lax
from jax.experimental import pallas as pl
from jax.experimental.pallas import tpu as pltpu

EPS = 1e-5
K_CONV = 31
HEAD_DIM = 64
KV_GROUP = 8
WINDOW = 128
TOP_K = 4
SWIGLU_LIMIT = 7.0
SWIGLU_ALPHA = 1.702

LANES = 128
SUBLANES = 8
HALO = 32
V7X_VMEM_BYTES = 64 * 1024 * 1024
VMEM_LIMIT = 56 * 1024 * 1024
VMEM_LIMIT_MOE = V7X_VMEM_BYTES - 2 * 1024 * 1024

F32 = jnp.float32
BF16 = jnp.bfloat16


def _const_spec(shape):
    nd = len(shape)
    return pl.BlockSpec(shape, lambda *_: (0,) * nd, pipeline_mode=pl.Buffered(1))


def _inproj_kernel(x_ref, g_ref, w_ref, b_ref, u_ref, q_ref, kv_ref, gate_ref, *, c_conv, n_q, n_kv, d_model):
    x = x_ref[...]
    ms = jnp.mean(x * x, axis=-1, keepdims=True)
    h = (x * lax.rsqrt(ms + EPS) * g_ref[...]).astype(BF16)
    cw = 512

    def seg(lo, width):
        return jnp.dot(h, w_ref[:, lo:lo + width], preferred_element_type=F32) + b_ref[:, lo:lo + width]

    for c0 in range(0, c_conv, cw):
        a = seg(c0, cw)
        b = seg(c_conv + c0, cw)
        u_ref[:, c0:c0 + cw] = (a * jax.nn.sigmoid(b)).astype(BF16)
    off = 2 * c_conv
    scale = 1.0 / math.sqrt(HEAD_DIM)
    for c0 in range(0, n_q, cw):
        q_ref[:, c0:c0 + cw] = (seg(off + c0, cw) * scale).astype(BF16)
    off += n_q
    kv_ref[...] = seg(off, 2 * n_kv).astype(BF16)
    off += 2 * n_kv
    for c0 in range(0, 2 * d_model, cw):
        gate_ref[:, c0:c0 + cw] = jax.nn.sigmoid(seg(off + c0, cw)).astype(BF16)


def _inproj(x2, g_mix, w_in, b_in, *, c_conv, n_q, n_kv, tm):
    n, d = x2.shape
    n_in = w_in.shape[1]
    kern = functools.partial(_inproj_kernel, c_conv=c_conv, n_q=n_q, n_kv=n_kv, d_model=d)
    return pl.pallas_call(
        kern,
        grid=(n // tm,),
        in_specs=[
            pl.BlockSpec((tm, d), lambda i: (i, 0)),
            _const_spec((1, d)),
            _const_spec((d, n_in)),
            _const_spec((1, n_in)),
        ],
        out_specs=[
            pl.BlockSpec((tm, c_conv), lambda i: (i, 0)),
            pl.BlockSpec((tm, n_q), lambda i: (i, 0)),
            pl.BlockSpec((tm, 2 * n_kv), lambda i: (i, 0)),
            pl.BlockSpec((tm, 2 * d), lambda i: (i, 0)),
        ],
        out_shape=[
            jax.ShapeDtypeStruct((n, c_conv), BF16),
            jax.ShapeDtypeStruct((n, n_q), BF16),
            jax.ShapeDtypeStruct((n, 2 * n_kv), BF16),
            jax.ShapeDtypeStruct((n, 2 * d), BF16),
        ],
        compiler_params=pltpu.CompilerParams(dimension_semantics=("arbitrary",), vmem_limit_bytes=VMEM_LIMIT),
        name="inproj",
    )(x2, g_mix.reshape(1, d), w_in.astype(BF16), b_in.reshape(1, n_in))


def _conv_kernel(halo_ref, u_ref, w_ref, cb_ref, lg_ref, lb_ref, act_ref, ext_ref, conv_ref, *, tt, strip):
    i = pl.program_id(1)
    rows = tt + HALO
    halo = jnp.where(i > 0, halo_ref[...].astype(F32), 0.0)
    ext = jnp.concatenate([halo, u_ref[...].astype(F32)], axis=0)
    ext_ref[0] = ext
    for s in range(1, SUBLANES):
        ext_ref[s] = pltpu.roll(ext, rows - s, axis=0)

    rgrp = 8
    rchunk = rgrp * SUBLANES
    for cb in range(ext.shape[1] // LANES):
        lanes = slice(cb * LANES, (cb + 1) * LANES)

        def taps(rc, carry, lanes=lanes):
            base = pl.multiple_of(rc * rchunk, rchunk)
            accs = [jnp.zeros((SUBLANES, LANES), F32) for _ in range(rgrp)]
            for j in range(K_CONV):
                a, s = divmod(HALO - (K_CONV - 1) + j, SUBLANES)
                wv = w_ref[j, :, lanes]
                for g in range(rgrp):
                    accs[g] = accs[g] + ext_ref[s, pl.ds(base + SUBLANES * (a + g), SUBLANES), lanes] * wv
            for g in range(rgrp):
                conv_ref[pl.ds(base + SUBLANES * g, SUBLANES), lanes] = accs[g]
            return carry

        lax.fori_loop(0, tt // rchunk, taps, 0)

    for r in range(tt // strip):
        rows_r = slice(r * strip, (r + 1) * strip)
        v = conv_ref[rows_r, :] + cb_ref[...]
        mu = jnp.mean(v, axis=-1, keepdims=True)
        dv = v - mu
        var = jnp.mean(dv * dv, axis=-1, keepdims=True)
        y = dv * lax.rsqrt(var + EPS) * lg_ref[...] + lb_ref[...]
        act_ref[rows_r, :] = (y * jax.nn.sigmoid(y)).astype(BF16)


def _conv(u3, conv_w, conv_b, ln_g, ln_b, *, tt):
    b, t, c = u3.shape
    hb = tt // HALO
    kern = functools.partial(_conv_kernel, tt=tt, strip=32)
    wpad = jnp.broadcast_to(conv_w[:, None, :], (K_CONV, SUBLANES, c))
    return pl.pallas_call(
        kern,
        grid=(b, t // tt),
        in_specs=[
            pl.BlockSpec((None, HALO, c), lambda bi, i: (bi, jnp.maximum(i * hb - 1, 0), 0)),
            pl.BlockSpec((None, tt, c), lambda bi, i: (bi, i, 0)),
            _const_spec((K_CONV, SUBLANES, c)),
            _const_spec((1, c)),
            _const_spec((1, c)),
            _const_spec((1, c)),
        ],
        out_specs=pl.BlockSpec((None, tt, c), lambda bi, i: (bi, i, 0)),
        out_shape=jax.ShapeDtypeStruct((b, t, c), BF16),
        scratch_shapes=[pltpu.VMEM((SUBLANES, tt + HALO, c), F32), pltpu.VMEM((tt, c), F32)],
        compiler_params=pltpu.CompilerParams(dimension_semantics=("arbitrary", "arbitrary"),
                                             vmem_limit_bytes=VMEM_LIMIT),
        name="conv",
    )(u3, u3, wpad, conv_b.reshape(1, c), ln_g.reshape(1, c), ln_b.reshape(1, c))


def _attn_kernel(q_ref, kvp_ref, kvc_ref, bias_ref, sink_ref, o_ref):
    i = pl.program_id(1)
    qb = WINDOW
    lane = lax.broadcasted_iota(jnp.int32, (2 * qb, LANES), 1)
    lo_half = lane < HEAD_DIM
    kvp = kvp_ref[...].astype(F32)
    kvc = kvc_ref[...].astype(F32)
    kband = jnp.concatenate([kvp[:, :LANES], kvc[:, :LANES]], axis=0)
    vband = jnp.concatenate([kvp[:, LANES:], kvc[:, LANES:]], axis=0)
    kswap = pltpu.roll(kband, HEAD_DIM, axis=1)
    vswap = pltpu.roll(vband, HEAD_DIM, axis=1)
    col = lax.broadcasted_iota(jnp.int32, (qb, 2 * qb), 1)
    first_block_mask = jnp.where((i == 0) & (col < qb), -jnp.inf, 0.0).astype(F32)
    ind_r = lax.broadcasted_iota(jnp.int32, (4 * qb, LANES), 0) < 2 * qb
    ind_c = lax.broadcasted_iota(jnp.int32, (4 * qb, LANES), 1) < HEAD_DIM
    ind = jnp.where(ind_r == ind_c, 1.0, 0.0).astype(BF16)
    lo_lane = lax.broadcasted_iota(jnp.int32, (qb, LANES), 1) < HEAD_DIM

    for h in range(2):
        ksrc_lo, ksrc_hi = (kband, kswap) if h == 0 else (kswap, kband)
        vsrc_lo, vsrc_hi = (vband, vswap) if h == 0 else (vswap, vband)
        kext = jnp.concatenate([jnp.where(lo_half, ksrc_lo, 0.0), jnp.where(lo_half, 0.0, ksrc_hi)],
                               axis=0).astype(BF16)
        vext = jnp.concatenate([jnp.where(lo_half, vsrc_lo, 0.0), jnp.where(lo_half, 0.0, vsrc_hi)],
                               axis=0).astype(BF16)
        vext2 = jnp.concatenate([vext, ind], axis=1)
        for j in range(4):
            cols = slice((4 * h + j) * LANES, (4 * h + j + 1) * LANES)
            s = lax.dot_general(q_ref[:, cols], kext, (((1,), (1,)), ((), ())),
                                preferred_element_type=F32)
            ps, es = [], []
            for half in range(2):
                sh = s[:, half * 2 * qb:(half + 1) * 2 * qb] + bias_ref[h, half, j] + first_block_mask
                sink = sink_ref[8 * h + 2 * j + half]
                m = jnp.maximum(jnp.max(sh, axis=-1, keepdims=True), sink)
                ps.append(jnp.exp(sh - m).astype(BF16))
                es.append(jnp.exp(sink - m))
            r = jnp.dot(jnp.concatenate(ps, axis=1), vext2, preferred_element_type=F32)
            den = r[:, LANES:] + jnp.where(lo_lane, es[0], es[1])
            o_ref[:, cols] = (r[:, :LANES] / den).astype(BF16)


def _alibi_bias(n_heads):
    qb = WINDOW
    slopes = jnp.asarray([2.0 ** (-8.0 * (i + 1) / n_heads) for i in range(n_heads)], dtype=F32)
    qi = jnp.arange(qb)[:, None]
    sj = jnp.arange(2 * qb)[None, :]
    dist = (qi + qb - sj).astype(F32)
    valid = (dist >= 0) & (dist < WINDOW)
    heads = (8 * jnp.arange(2)[:, None, None] + jnp.arange(2)[None, :, None] + 2 * jnp.arange(4)[None, None, :])
    sl = slopes[heads]
    return jnp.where(valid[None, None, None], -sl[..., None, None] * dist, -jnp.inf)


def _attn(q3, kv3, attn_sinks, n_heads):
    b, t, nq = q3.shape
    qb = WINDOW
    bias = _alibi_bias(n_heads)
    return pl.pallas_call(
        _attn_kernel,
        grid=(b, t // qb),
        in_specs=[
            pl.BlockSpec((None, qb, nq), lambda bi, i: (bi, i, 0)),
            pl.BlockSpec((None, qb, kv3.shape[2]), lambda bi, i: (bi, jnp.maximum(i - 1, 0), 0)),
            pl.BlockSpec((None, qb, kv3.shape[2]), lambda bi, i: (bi, i, 0)),
            _const_spec(bias.shape),
            pl.BlockSpec(memory_space=pltpu.SMEM),
        ],
        out_specs=pl.BlockSpec((None, qb, nq), lambda bi, i: (bi, i, 0)),
        out_shape=jax.ShapeDtypeStruct((b, t, nq), BF16),
        compiler_params=pltpu.CompilerParams(dimension_semantics=("arbitrary", "arbitrary"),
                                             vmem_limit_bytes=VMEM_LIMIT),
        name="attn",
    )(q3, kv3, kv3, bias, attn_sinks.astype(F32))


def _merge_kernel(x_ref, act_ref, o_ref, gate_ref, wco_ref, bco_ref, wao_ref, bao_ref, wout_ref,
                  gffn_ref, wr_ref, br_ref,
                  x1_ref, hr_ref, ri_ref, rw_ref, cnt_ref,
                  carry_ref, *, tm, d, n_exp, tiles_per_chunk):
    i = pl.program_id(0)
    yc = jnp.dot(act_ref[...], wco_ref[...], preferred_element_type=F32) + bco_ref[...]
    ya = jnp.dot(o_ref[...], wao_ref[...], preferred_element_type=F32) + bao_ref[...]
    m = gate_ref[:, :d].astype(F32) * yc + gate_ref[:, d:].astype(F32) * ya
    x1 = x_ref[...] + jnp.dot(m.astype(BF16), wout_ref[...], preferred_element_type=F32)
    x1_ref[...] = x1
    ms = jnp.mean(x1 * x1, axis=-1, keepdims=True)
    h2 = x1 * lax.rsqrt(ms + EPS) * gffn_ref[...]
    nblk = d // LANES
    for cb in range(nblk):
        hr_ref[pl.ds(cb, tm, stride=nblk), :] = h2[:, cb * LANES:(cb + 1) * LANES]

    h_hi = h2.astype(BF16)
    h_lo = (h2 - h_hi.astype(F32)).astype(BF16)
    wr = wr_ref[...]
    w_hi = wr.astype(BF16)
    w_lo = (wr - w_hi.astype(F32)).astype(BF16)
    logits = (jnp.dot(h_hi, w_hi, preferred_element_type=F32) + jnp.dot(h_hi, w_lo, preferred_element_type=F32)
              + jnp.dot(h_lo, w_hi, preferred_element_type=F32)) + br_ref[...]

    iota = lax.broadcasted_iota(jnp.int32, (tm, n_exp), 1)
    l = logits
    vals, idxs, sels = [], [], []
    for _ in range(TOP_K):
        mk = jnp.max(l, axis=-1, keepdims=True)
        ik = jnp.min(jnp.where(l == mk, iota, n_exp), axis=-1, keepdims=True)
        sel = iota == ik
        l = jnp.where(sel, -jnp.inf, l)
        vals.append(mk)
        idxs.append(ik)
        sels.append(sel)
    exs = [jnp.exp(v - vals[0]) for v in vals]
    den = exs[0] + exs[1] + exs[2] + exs[3]

    @pl.when(i % tiles_per_chunk == 0)
    def _():
        carry_ref[...] = jnp.zeros_like(carry_ref)

    member = (sels[0] | sels[1] | sels[2] | sels[3])
    mf = jnp.where(member, 1.0, 0.0).astype(F32)
    r_i = lax.broadcasted_iota(jnp.int32, (tm, tm), 0)
    c_i = lax.broadcasted_iota(jnp.int32, (tm, tm), 1)
    tri = jnp.where(c_i < r_i, 1.0, 0.0).astype(BF16)
    before = jnp.dot(tri, mf.astype(BF16), preferred_element_type=F32) + carry_ref[...]
    carry_new = carry_ref[...] + jnp.sum(mf, axis=0, keepdims=True)
    carry_ref[...] = carry_new
    cnt_ref[...] = carry_new.astype(jnp.int32)

    io8 = lax.broadcasted_iota(jnp.int32, (tm, 2 * TOP_K), 1)
    ri = jnp.zeros((tm, 2 * TOP_K), jnp.int32)
    rw = jnp.zeros((tm, TOP_K), F32)
    io4 = lax.broadcasted_iota(jnp.int32, (tm, TOP_K), 1)
    for k in range(TOP_K):
        rank = jnp.sum(jnp.where(sels[k], before, 0.0), axis=-1, keepdims=True).astype(jnp.int32)
        ri = jnp.where(io8 == k, idxs[k], ri)
        ri = jnp.where(io8 == TOP_K + k, rank, ri)
        rw = jnp.where(io4 == k, exs[k] / den, rw)
    ri_ref[...] = ri
    rw_ref[...] = rw


def _merge(x2, act, o, gates, w_conv_out, b_conv_out, w_attn_out, b_attn_out, w_out, g_ffn, w_router, b_router,
           *, tm, chunk):
    n, d = x2.shape
    n_exp = w_router.shape[1]
    tiles_per_chunk = chunk // tm
    n_chunks = n // chunk
    nblk = d // LANES
    kern = functools.partial(_merge_kernel, tm=tm, d=d, n_exp=n_exp, tiles_per_chunk=tiles_per_chunk)
    row = lambda i: (i, 0)
    return pl.pallas_call(
        kern,
        grid=(n // tm,),
        in_specs=[
            pl.BlockSpec((tm, d), row),
            pl.BlockSpec((tm, d), row),
            pl.BlockSpec((tm, d), row),
            pl.BlockSpec((tm, 2 * d), row),
            _const_spec((d, d)), _const_spec((1, d)),
            _const_spec((d, d)), _const_spec((1, d)),
            _const_spec((d, d)),
            _const_spec((1, d)),
            _const_spec((d, n_exp)), _const_spec((1, n_exp)),
        ],
        out_specs=[
            pl.BlockSpec((tm, d), row),
            pl.BlockSpec((tm * nblk, LANES), row),
            pl.BlockSpec((tm, 2 * TOP_K), row),
            pl.BlockSpec((tm, TOP_K), row),
            pl.BlockSpec((None, 1, n_exp), lambda i: (i // tiles_per_chunk, 0, 0)),
        ],
        out_shape=[
            jax.ShapeDtypeStruct((n, d), F32),
            jax.ShapeDtypeStruct((n * nblk, LANES), F32),
            jax.ShapeDtypeStruct((n, 2 * TOP_K), jnp.int32),
            jax.ShapeDtypeStruct((n, TOP_K), F32),
            jax.ShapeDtypeStruct((n_chunks, 1, n_exp), jnp.int32),
        ],
        scratch_shapes=[pltpu.VMEM((1, n_exp), F32)],
        compiler_params=pltpu.CompilerParams(dimension_semantics=("arbitrary",), vmem_limit_bytes=VMEM_LIMIT),
        name="merge",
    )(x2, act, o, gates, w_conv_out.astype(BF16), b_conv_out.reshape(1, d), w_attn_out.astype(BF16),
      b_attn_out.reshape(1, d), w_out.astype(BF16), g_ffn.reshape(1, d), w_router, b_router.reshape(1, n_exp))


def _moe_kernel(cnt_ref, pos_ref, w_ref, h_ref, wg_ref, bg_ref, wu_ref, bu_ref, wd_ref, bd_ref, x1_ref, gf_ref,
                out_ref, xs_ref, off_ref, ys_ref, *, chunk, tsub, d, n_exp, bm):
    c = pl.program_id(0)
    s = pl.program_id(1)
    nblk = d // LANES
    n_sub = chunk // tsub

    @pl.when((c == 0) & (s == 0))
    def _():
        xs_ref[...] = jnp.zeros_like(xs_ref)

    @pl.when(s == 0)
    def _():
        def offs(j, acc):
            off_ref[j] = acc
            return acc + cnt_ref[c, j]

        lax.fori_loop(0, n_exp, offs, 0)

    @pl.when(s < n_sub)
    def _():
        a0 = s * tsub * TOP_K

        def dispatch(i, carry):
            row = h_ref[pl.ds(pl.multiple_of(i * nblk, nblk), nblk), :]
            for k in range(TOP_K):
                p = pos_ref[a0 + i * TOP_K + k]
                xs_ref[pl.ds(pl.multiple_of(p * nblk, nblk), nblk), :] = row
            return carry

        lax.fori_loop(0, tsub, dispatch, 0, unroll=4)

    @pl.when((s >= n_sub) & (s < n_sub + n_exp))
    def _():
        e = s - n_sub
        cnt = cnt_ref[c, e]
        base = off_ref[e]
        row_id = lax.broadcasted_iota(jnp.int32, (bm, LANES), 0)

        def block(b, carry):
            row0 = (base + b * bm) * nblk
            x = jnp.concatenate([xs_ref[pl.ds(row0 + cb, bm, stride=nblk), :].astype(BF16) for cb in range(nblk)],
                                axis=1)
            g = jnp.dot(x, wg_ref[...], preferred_element_type=F32) + bg_ref[...]
            u = jnp.dot(x, wu_ref[...], preferred_element_type=F32) + bu_ref[...]
            g = jnp.minimum(g, SWIGLU_LIMIT)
            u = jnp.clip(u, -SWIGLU_LIMIT, SWIGLU_LIMIT)
            a = (u + 1.0) * (g * jax.nn.sigmoid(SWIGLU_ALPHA * g))
            y = jnp.dot(a.astype(BF16), wd_ref[...], preferred_element_type=F32) + bd_ref[...]
            keep = row_id < cnt - b * bm
            for cb in range(nblk):
                dst = pl.ds(row0 + cb, bm, stride=nblk)
                xs_ref[dst, :] = jnp.where(keep, y[:, cb * LANES:(cb + 1) * LANES], xs_ref[dst, :])
            return carry

        lax.fori_loop(0, (cnt + bm - 1) // bm, block, 0)

    @pl.when(s >= n_sub + n_exp)
    def _():
        a0 = (s - n_sub - n_exp) * tsub * TOP_K

        def combine(i, carry):
            acc = jnp.zeros((nblk, LANES), F32)
            for k in range(TOP_K):
                a = a0 + i * TOP_K + k
                acc = acc + w_ref[a] * xs_ref[pl.ds(pl.multiple_of(pos_ref[a] * nblk, nblk), nblk), :]
            ys_ref[pl.ds(pl.multiple_of(i * nblk, nblk), nblk), :] = acc
            return carry

        lax.fori_loop(0, tsub, combine, 0, unroll=4)
        y = jnp.concatenate([ys_ref[pl.ds(cb, tsub, stride=nblk), :] for cb in range(nblk)], axis=1)
        x = x1_ref[...] + y
        ms = jnp.mean(x * x, axis=-1, keepdims=True)
        out_ref[...] = x * lax.rsqrt(ms + EPS) * gf_ref[...]


def _moe(counts, pos_flat, w_flat, hr, wg, bg, wu, bu, wd, bd, x1, g_final, *, chunk, tsub, d, bm):
    n_chunks, n_exp = counts.shape
    nblk = d // LANES
    f = wg.shape[2]
    n_assign = chunk * TOP_K
    n_sub = chunk // tsub
    kern = functools.partial(_moe_kernel, chunk=chunk, tsub=tsub, d=d, n_exp=n_exp, bm=bm)
    smem_blk = lambda: pl.BlockSpec((n_assign,), lambda c, s, cnt: (c,), memory_space=pltpu.SMEM)
    wspec = lambda shp: pl.BlockSpec((None,) + shp, lambda c, s, cnt: (jnp.clip(s - n_sub, 0, n_exp - 1), 0, 0))
    out_tile = lambda c, s, cnt: (c * n_sub + jnp.clip(s - n_sub - n_exp, 0, n_sub - 1), 0)
    grid_spec = pltpu.PrefetchScalarGridSpec(
        num_scalar_prefetch=1,
        grid=(n_chunks, 2 * n_sub + n_exp),
        in_specs=[
            smem_blk(), smem_blk(),
            pl.BlockSpec((tsub * nblk, LANES), lambda c, s, cnt: (c * n_sub + jnp.minimum(s, n_sub - 1), 0)),
            wspec((d, f)), wspec((1, f)), wspec((d, f)), wspec((1, f)), wspec((f, d)), wspec((1, d)),
            pl.BlockSpec((tsub, d), out_tile),
            pl.BlockSpec((1, d), lambda c, s, cnt: (0, 0)),
        ],
        out_specs=pl.BlockSpec((tsub, d), out_tile),
        scratch_shapes=[
            pltpu.VMEM(((n_assign + bm) * nblk, LANES), F32),
            pltpu.SMEM((n_exp,), jnp.int32),
            pltpu.VMEM((tsub * nblk, LANES), F32),
        ],
    )
    return pl.pallas_call(
        kern,
        grid_spec=grid_spec,
        out_shape=jax.ShapeDtypeStruct((n_chunks * chunk, d), F32),
        compiler_params=pltpu.CompilerParams(dimension_semantics=("arbitrary", "arbitrary"),
                                             vmem_limit_bytes=VMEM_LIMIT_MOE),
        name="moe",
    )(counts, pos_flat, w_flat, hr, wg, bg, wu, bu, wd, bd, x1, g_final.reshape(1, d))


def _pos_kernel(cnt_ref, e_ref, r_ref, pos_ref, *, n_exp):
    cnt = jnp.broadcast_to(cnt_ref[...], (SUBLANES, n_exp))
    hi = (cnt // 256).astype(F32).astype(BF16)
    lo = (cnt % 256).astype(F32).astype(BF16)
    r_i = lax.broadcasted_iota(jnp.int32, (n_exp, n_exp), 0)
    c_i = lax.broadcasted_iota(jnp.int32, (n_exp, n_exp), 1)
    tri = jnp.where(r_i < c_i, 1.0, 0.0).astype(BF16)
    off = (256.0 * jnp.dot(hi, tri, preferred_element_type=F32)
           + jnp.dot(lo, tri, preferred_element_type=F32)).astype(jnp.int32)[0:1]
    e = e_ref[...]
    pos = r_ref[...]
    for j in range(n_exp):
        pos = pos + jnp.where(e == j, off[:, j:j + 1], 0)
    pos_ref[...] = pos


def _pos(counts, e3, r3):
    n_chunks, rows, _ = e3.shape
    n_exp = counts.shape[-1]
    assert rows * LANES // TOP_K < 256 * 256
    blk = pl.BlockSpec((None, rows, LANES), lambda c: (c, 0, 0))
    return pl.pallas_call(
        functools.partial(_pos_kernel, n_exp=n_exp),
        grid=(n_chunks,),
        in_specs=[pl.BlockSpec((None, 1, n_exp), lambda c: (c, 0, 0)), blk, blk],
        out_specs=blk,
        out_shape=jax.ShapeDtypeStruct(e3.shape, jnp.int32),
        compiler_params=pltpu.CompilerParams(dimension_semantics=("arbitrary",), vmem_limit_bytes=VMEM_LIMIT),
        name="pos",
    )(counts, e3, r3)


def _block_sizes(batch, seq):
    n = batch * seq
    tm = min(512, n)
    tt = min(256, seq)
    chunk = min(2048, n)
    return tm, tt, chunk


def kernel(x, g_mix, w_in, b_in, conv_w, conv_b, conv_ln_g, conv_ln_b, w_conv_out, b_conv_out, attn_sinks,
           w_attn_out, b_attn_out, w_out, g_ffn, w_router, b_router, w_gate, b_gate, w_up, b_up, w_down,
           b_down, g_final):
    batch, seq, d = x.shape
    n = batch * seq
    c_conv = conv_w.shape[1]
    n_heads = attn_sinks.shape[0]
    n_q = n_heads * HEAD_DIM
    n_kv = (n_heads // KV_GROUP) * HEAD_DIM
    n_exp = w_router.shape[1]
    assert n_kv == LANES and seq % WINDOW == 0 and d % LANES == 0
    tm, tt, chunk = _block_sizes(batch, seq)
    assert n % tm == 0 and seq % tt == 0 and n % chunk == 0 and chunk % tm == 0

    x2 = x.reshape(n, d)
    u, q, kv, gates = _inproj(x2, g_mix, w_in, b_in, c_conv=c_conv, n_q=n_q, n_kv=n_kv, tm=tm)
    act = _conv(u.reshape(batch, seq, c_conv), conv_w, conv_b, conv_ln_g, conv_ln_b, tt=tt)
    o = _attn(q.reshape(batch, seq, n_q), kv.reshape(batch, seq, 2 * n_kv), attn_sinks, n_heads)
    x1, hr, ri, rw, counts = _merge(
        x2, act.reshape(n, c_conv), o.reshape(n, n_q), gates, w_conv_out, b_conv_out, w_attn_out, b_attn_out,
        w_out, g_ffn, w_router, b_router, tm=tm, chunk=chunk)
    n_chunks = n // chunk
    rows = chunk * TOP_K // LANES
    pos = _pos(counts, ri[:, :TOP_K].reshape(n_chunks, rows, LANES), ri[:, TOP_K:].reshape(n_chunks, rows, LANES))
    out = _moe(counts.reshape(n_chunks, n_exp), pos.reshape(-1), rw.reshape(-1), hr,
               w_gate.astype(BF16), b_gate.reshape(n_exp, 1, -1), w_up.astype(BF16), b_up.reshape(n_exp, 1, -1),
               w_down.astype(BF16), b_down.reshape(n_exp, 1, -1), x1, g_final, chunk=chunk, tsub=tm, d=d, bm=128)
    return out.reshape(batch, seq, d)
```

```python
import functools
import math

import jax
import jax.numpy as jnp
from jax import lax
from jax.experimental import pallas as pl
from jax.experimental.pallas import tpu as pltpu

EPS = 1e-5
K_CONV = 31
HEAD_DIM = 64
KV_GROUP = 8
WINDOW = 128
TOP_K = 4
SWIGLU_LIMIT = 7.0
SWIGLU_ALPHA = 1.702

LANES = 128
SUBLANES = 8
HALO = 32
BLOCK_STEP = 32
V7X_VMEM_BYTES = 64 * 1024 * 1024
VMEM_LIMIT = 56 * 1024 * 1024
VMEM_LIMIT_MOE = V7X_VMEM_BYTES - 2 * 1024 * 1024

F32 = jnp.float32
BF16 = jnp.bfloat16


def _const_spec(shape):
    nd = len(shape)
    return pl.BlockSpec(shape, lambda *_: (0,) * nd, pipeline_mode=pl.Buffered(1))


def _inproj_kernel(x_ref, g_ref, w_ref, b_ref, wg_ref, wu_ref, wd_ref,
                   u_ref, q_ref, kv_ref, gate_ref, wgb_ref, wub_ref, wdb_ref, *, c_conv, n_q, n_kv, d_model):
    wgb_ref[...] = wg_ref[...].astype(BF16)
    wub_ref[...] = wu_ref[...].astype(BF16)
    wdb_ref[...] = wd_ref[...].astype(BF16)

    x = x_ref[...]
    ms = jnp.mean(x * x, axis=-1, keepdims=True)
    h = (x * lax.rsqrt(ms + EPS) * g_ref[...]).astype(BF16)
    cw = 512

    def seg(lo, width):
        return jnp.dot(h, w_ref[:, lo:lo + width], preferred_element_type=F32) + b_ref[:, lo:lo + width]

    for c0 in range(0, c_conv, cw):
        a = seg(c0, cw)
        b = seg(c_conv + c0, cw)
        u_ref[:, c0:c0 + cw] = (a * jax.nn.sigmoid(b)).astype(BF16)
    off = 2 * c_conv
    scale = 1.0 / math.sqrt(HEAD_DIM)
    for c0 in range(0, n_q, cw):
        q_ref[:, c0:c0 + cw] = (seg(off + c0, cw) * scale).astype(BF16)
    off += n_q
    kv_ref[...] = seg(off, 2 * n_kv).astype(BF16)
    off += 2 * n_kv
    for c0 in range(0, 2 * d_model, cw):
        gate_ref[:, c0:c0 + cw] = jax.nn.sigmoid(seg(off + c0, cw)).astype(BF16)


def _inproj(x2, g_mix, w_in, b_in, w_gate, w_up, w_down, *, c_conv, n_q, n_kv, tm):
    n, d = x2.shape
    n_in = w_in.shape[1]
    steps = n // tm
    slabs = []
    for w in (w_gate, w_up, w_down):
        rows = w.shape[0] * w.shape[1]
        assert rows % steps == 0 and (rows // steps) % 16 == 0
        slabs.append((rows // steps, w.shape[2]))
    kern = functools.partial(_inproj_kernel, c_conv=c_conv, n_q=n_q, n_kv=n_kv, d_model=d)
    row = lambda i: (i, 0)
    outs = pl.pallas_call(
        kern,
        grid=(steps,),
        in_specs=[
            pl.BlockSpec((tm, d), row),
            _const_spec((1, d)),
            _const_spec((d, n_in)),
            _const_spec((1, n_in)),
        ] + [pl.BlockSpec(sl, row) for sl in slabs],
        out_specs=[
            pl.BlockSpec((tm, c_conv), row),
            pl.BlockSpec((tm, n_q), row),
            pl.BlockSpec((tm, 2 * n_kv), row),
            pl.BlockSpec((tm, 2 * d), row),
        ] + [pl.BlockSpec(sl, row) for sl in slabs],
        out_shape=[
            jax.ShapeDtypeStruct((n, c_conv), BF16),
            jax.ShapeDtypeStruct((n, n_q), BF16),
            jax.ShapeDtypeStruct((n, 2 * n_kv), BF16),
            jax.ShapeDtypeStruct((n, 2 * d), BF16),
        ] + [jax.ShapeDtypeStruct((w.shape[0] * w.shape[1], w.shape[2]), BF16) for w in (w_gate, w_up, w_down)],
        compiler_params=pltpu.CompilerParams(dimension_semantics=("arbitrary",), vmem_limit_bytes=VMEM_LIMIT),
        name="inproj",
    )(x2, g_mix.reshape(1, d), w_in.astype(BF16), b_in.reshape(1, n_in),
      *[w.reshape(w.shape[0] * w.shape[1], w.shape[2]) for w in (w_gate, w_up, w_down)])
    u, q, kv, gates = outs[:4]
    wgb, wub, wdb = [o.reshape(w.shape) for o, w in zip(outs[4:], (w_gate, w_up, w_down))]
    return u, q, kv, gates, wgb, wub, wdb


def _conv_kernel(halo_ref, u_ref, w_ref, cb_ref, lg_ref, lb_ref, act_ref, ext_ref, conv_ref, *, tt, strip):
    i = pl.program_id(1)
    rows = tt + HALO
    halo = jnp.where(i > 0, halo_ref[...].astype(F32), 0.0)
    ext = jnp.concatenate([halo, u_ref[...].astype(F32)], axis=0)
    ext_ref[0] = ext
    for s in range(1, SUBLANES):
        ext_ref[s] = pltpu.roll(ext, rows - s, axis=0)

    rgrp = 8
    rchunk = rgrp * SUBLANES
    for cb in range(ext.shape[1] // LANES):
        lanes = slice(cb * LANES, (cb + 1) * LANES)

        def taps(rc, carry, lanes=lanes):
            base = pl.multiple_of(rc * rchunk, rchunk)
            accs = [jnp.zeros((SUBLANES, LANES), F32) for _ in range(rgrp)]
            for j in range(K_CONV):
                a, s = divmod(HALO - (K_CONV - 1) + j, SUBLANES)
                wv = w_ref[j, :, lanes]
                for g in range(rgrp):
                    accs[g] = accs[g] + ext_ref[s, pl.ds(base + SUBLANES * (a + g), SUBLANES), lanes] * wv
            for g in range(rgrp):
                conv_ref[pl.ds(base + SUBLANES * g, SUBLANES), lanes] = accs[g]
            return carry

        lax.fori_loop(0, tt // rchunk, taps, 0)

    for r in range(tt // strip):
        rows_r = slice(r * strip, (r + 1) * strip)
        v = conv_ref[rows_r, :] + cb_ref[...]
        mu = jnp.mean(v, axis=-1, keepdims=True)
        dv = v - mu
        var = jnp.mean(dv * dv, axis=-1, keepdims=True)
        y = dv * lax.rsqrt(var + EPS) * lg_ref[...] + lb_ref[...]
        act_ref[rows_r, :] = (y * jax.nn.sigmoid(y)).astype(BF16)


def _conv(u3, conv_w, conv_b, ln_g, ln_b, *, tt):
    b, t, c = u3.shape
    hb = tt // HALO
    kern = functools.partial(_conv_kernel, tt=tt, strip=32)
    wpad = jnp.broadcast_to(conv_w[:, None, :], (K_CONV, SUBLANES, c))
    return pl.pallas_call(
        kern,
        grid=(b, t // tt),
        in_specs=[
            pl.BlockSpec((None, HALO, c), lambda bi, i: (bi, jnp.maximum(i * hb - 1, 0), 0)),
            pl.BlockSpec((None, tt, c), lambda bi, i: (bi, i, 0)),
            _const_spec((K_CONV, SUBLANES, c)),
            _const_spec((1, c)),
            _const_spec((1, c)),
            _const_spec((1, c)),
        ],
        out_specs=pl.BlockSpec((None, tt, c), lambda bi, i: (bi, i, 0)),
        out_shape=jax.ShapeDtypeStruct((b, t, c), BF16),
        scratch_shapes=[pltpu.VMEM((SUBLANES, tt + HALO, c), F32), pltpu.VMEM((tt, c), F32)],
        compiler_params=pltpu.CompilerParams(dimension_semantics=("arbitrary", "arbitrary"),
                                             vmem_limit_bytes=VMEM_LIMIT),
        name="conv",
    )(u3, u3, wpad, conv_b.reshape(1, c), ln_g.reshape(1, c), ln_b.reshape(1, c))


def _attn_kernel(q_ref, kvp_ref, kvc_ref, bias_ref, sink_ref, o_ref):
    i = pl.program_id(1)
    qb = WINDOW
    lane = lax.broadcasted_iota(jnp.int32, (2 * qb, LANES), 1)
    lo_half = lane < HEAD_DIM
    kvp = kvp_ref[...].astype(F32)
    kvc = kvc_ref[...].astype(F32)
    kband = jnp.concatenate([kvp[:, :LANES], kvc[:, :LANES]], axis=0)
    vband = jnp.concatenate([kvp[:, LANES:], kvc[:, LANES:]], axis=0)
    kswap = pltpu.roll(kband, HEAD_DIM, axis=1)
    vswap = pltpu.roll(vband, HEAD_DIM, axis=1)
    col = lax.broadcasted_iota(jnp.int32, (qb, 2 * qb), 1)
    first_block_mask = jnp.where((i == 0) & (col < qb), -jnp.inf, 0.0).astype(F32)
    ind_r = lax.broadcasted_iota(jnp.int32, (4 * qb, LANES), 0) < 2 * qb
    ind_c = lax.broadcasted_iota(jnp.int32, (4 * qb, LANES), 1) < HEAD_DIM
    ind = jnp.where(ind_r == ind_c, 1.0, 0.0).astype(BF16)
    lo_lane = lax.broadcasted_iota(jnp.int32, (qb, LANES), 1) < HEAD_DIM

    for h in range(2):
        ksrc_lo, ksrc_hi = (kband, kswap) if h == 0 else (kswap, kband)
        vsrc_lo, vsrc_hi = (vband, vswap) if h == 0 else (vswap, vband)
        kext = jnp.concatenate([jnp.where(lo_half, ksrc_lo, 0.0), jnp.where(lo_half, 0.0, ksrc_hi)],
                               axis=0).astype(BF16)
        vext = jnp.concatenate([jnp.where(lo_half, vsrc_lo, 0.0), jnp.where(lo_half, 0.0, vsrc_hi)],
                               axis=0).astype(BF16)
        vext2 = jnp.concatenate([vext, ind], axis=1)
        for j in range(4):
            cols = slice((4 * h + j) * LANES, (4 * h + j + 1) * LANES)
            s = lax.dot_general(q_ref[:, cols], kext, (((1,), (1,)), ((), ())),
                                preferred_element_type=F32)
            ps, es = [], []
            for half in range(2):
                sh = s[:, half * 2 * qb:(half + 1) * 2 * qb] + bias_ref[h, half, j] + first_block_mask
                sink = sink_ref[8 * h + 2 * j + half]
                m = jnp.maximum(jnp.max(sh, axis=-1, keepdims=True), sink)
                ps.append(jnp.exp(sh - m).astype(BF16))
                es.append(jnp.exp(sink - m))
            r = jnp.dot(jnp.concatenate(ps, axis=1), vext2, preferred_element_type=F32)
            den = r[:, LANES:] + jnp.where(lo_lane, es[0], es[1])
            o_ref[:, cols] = (r[:, :LANES] / den).astype(BF16)


def _alibi_bias(n_heads):
    qb = WINDOW
    slopes = jnp.asarray([2.0 ** (-8.0 * (i + 1) / n_heads) for i in range(n_heads)], dtype=F32)
    qi = jnp.arange(qb)[:, None]
    sj = jnp.arange(2 * qb)[None, :]
    dist = (qi + qb - sj).astype(F32)
    valid = (dist >= 0) & (dist < WINDOW)
    heads = (8 * jnp.arange(2)[:, None, None] + jnp.arange(2)[None, :, None] + 2 * jnp.arange(4)[None, None, :])
    sl = slopes[heads]
    return jnp.where(valid[None, None, None], -sl[..., None, None] * dist, -jnp.inf)


def _attn(q3, kv3, attn_sinks, n_heads):
    b, t, nq = q3.shape
    qb = WINDOW
    bias = _alibi_bias(n_heads)
    return pl.pallas_call(
        _attn_kernel,
        grid=(b, t // qb),
        in_specs=[
            pl.BlockSpec((None, qb, nq), lambda bi, i: (bi, i, 0)),
            pl.BlockSpec((None, qb, kv3.shape[2]), lambda bi, i: (bi, jnp.maximum(i - 1, 0), 0)),
            pl.BlockSpec((None, qb, kv3.shape[2]), lambda bi, i: (bi, i, 0)),
            _const_spec(bias.shape),
            pl.BlockSpec(memory_space=pltpu.SMEM),
        ],
        out_specs=pl.BlockSpec((None, qb, nq), lambda bi, i: (bi, i, 0)),
        out_shape=jax.ShapeDtypeStruct((b, t, nq), BF16),
        compiler_params=pltpu.CompilerParams(dimension_semantics=("arbitrary", "arbitrary"),
                                             vmem_limit_bytes=VMEM_LIMIT),
        name="attn",
    )(q3, kv3, kv3, bias, attn_sinks.astype(F32))


def _merge_kernel(x_ref, act_ref, o_ref, gate_ref, wco_ref, bco_ref, wao_ref, bao_ref, wout_ref,
                  gffn_ref, wr_ref, br_ref,
                  x1_ref, hr_ref, ri_ref, rw_ref, cnt_ref,
                  carry_ref, *, tm, d, n_exp, tiles_per_chunk):
    i = pl.program_id(0)
    yc = jnp.dot(act_ref[...], wco_ref[...], preferred_element_type=F32) + bco_ref[...]
    ya = jnp.dot(o_ref[...], wao_ref[...], preferred_element_type=F32) + bao_ref[...]
    m = gate_ref[:, :d].astype(F32) * yc + gate_ref[:, d:].astype(F32) * ya
    x1 = x_ref[...] + jnp.dot(m.astype(BF16), wout_ref[...], preferred_element_type=F32)
    x1_ref[...] = x1
    ms = jnp.mean(x1 * x1, axis=-1, keepdims=True)
    h2 = x1 * lax.rsqrt(ms + EPS) * gffn_ref[...]
    nblk = d // LANES
    for cb in range(nblk):
        hr_ref[pl.ds(cb, tm, stride=nblk), :] = h2[:, cb * LANES:(cb + 1) * LANES]

    h_hi = h2.astype(BF16)
    h_lo = (h2 - h_hi.astype(F32)).astype(BF16)
    wr = wr_ref[...]
    w_hi = wr.astype(BF16)
    w_lo = (wr - w_hi.astype(F32)).astype(BF16)
    logits = (jnp.dot(h_hi, w_hi, preferred_element_type=F32) + jnp.dot(h_hi, w_lo, preferred_element_type=F32)
              + jnp.dot(h_lo, w_hi, preferred_element_type=F32)) + br_ref[...]

    iota = lax.broadcasted_iota(jnp.int32, (tm, n_exp), 1)
    l = logits
    vals, idxs, sels = [], [], []
    for _ in range(TOP_K):
        mk = jnp.max(l, axis=-1, keepdims=True)
        ik = jnp.min(jnp.where(l == mk, iota, n_exp), axis=-1, keepdims=True)
        sel = iota == ik
        l = jnp.where(sel, -jnp.inf, l)
        vals.append(mk)
        idxs.append(ik)
        sels.append(sel)
    exs = [jnp.exp(v - vals[0]) for v in vals]
    den = exs[0] + exs[1] + exs[2] + exs[3]

    @pl.when(i % tiles_per_chunk == 0)
    def _():
        carry_ref[...] = jnp.zeros_like(carry_ref)

    member = (sels[0] | sels[1] | sels[2] | sels[3])
    mf = jnp.where(member, 1.0, 0.0).astype(F32)
    r_i = lax.broadcasted_iota(jnp.int32, (tm, tm), 0)
    c_i = lax.broadcasted_iota(jnp.int32, (tm, tm), 1)
    tri = jnp.where(c_i < r_i, 1.0, 0.0).astype(BF16)
    before = jnp.dot(tri, mf.astype(BF16), preferred_element_type=F32) + carry_ref[...]
    carry_new = carry_ref[...] + jnp.sum(mf, axis=0, keepdims=True)
    carry_ref[...] = carry_new
    cnt_ref[...] = carry_new.astype(jnp.int32)

    io8 = lax.broadcasted_iota(jnp.int32, (tm, 2 * TOP_K), 1)
    ri = jnp.zeros((tm, 2 * TOP_K), jnp.int32)
    rw = jnp.zeros((tm, TOP_K), F32)
    io4 = lax.broadcasted_iota(jnp.int32, (tm, TOP_K), 1)
    for k in range(TOP_K):
        rank = jnp.sum(jnp.where(sels[k], before, 0.0), axis=-1, keepdims=True).astype(jnp.int32)
        ri = jnp.where(io8 == k, idxs[k], ri)
        ri = jnp.where(io8 == TOP_K + k, rank, ri)
        rw = jnp.where(io4 == k, exs[k] / den, rw)
    ri_ref[...] = ri
    rw_ref[...] = rw


def _merge(x2, act, o, gates, w_conv_out, b_conv_out, w_attn_out, b_attn_out, w_out, g_ffn, w_router, b_router,
           *, tm, chunk):
    n, d = x2.shape
    n_exp = w_router.shape[1]
    tiles_per_chunk = chunk // tm
    n_chunks = n // chunk
    nblk = d // LANES
    kern = functools.partial(_merge_kernel, tm=tm, d=d, n_exp=n_exp, tiles_per_chunk=tiles_per_chunk)
    row = lambda i: (i, 0)
    return pl.pallas_call(
        kern,
        grid=(n // tm,),
        in_specs=[
            pl.BlockSpec((tm, d), row),
            pl.BlockSpec((tm, d), row),
            pl.BlockSpec((tm, d), row),
            pl.BlockSpec((tm, 2 * d), row),
            _const_spec((d, d)), _const_spec((1, d)),
            _const_spec((d, d)), _const_spec((1, d)),
            _const_spec((d, d)),
            _const_spec((1, d)),
            _const_spec((d, n_exp)), _const_spec((1, n_exp)),
        ],
        out_specs=[
            pl.BlockSpec((tm, d), row),
            pl.BlockSpec((tm * nblk, LANES), row),
            pl.BlockSpec((tm, 2 * TOP_K), row),
            pl.BlockSpec((tm, TOP_K), row),
            pl.BlockSpec((None, 1, n_exp), lambda i: (i // tiles_per_chunk, 0, 0)),
        ],
        out_shape=[
            jax.ShapeDtypeStruct((n, d), F32),
            jax.ShapeDtypeStruct((n * nblk, LANES), F32),
            jax.ShapeDtypeStruct((n, 2 * TOP_K), jnp.int32),
            jax.ShapeDtypeStruct((n, TOP_K), F32),
            jax.ShapeDtypeStruct((n_chunks, 1, n_exp), jnp.int32),
        ],
        scratch_shapes=[pltpu.VMEM((1, n_exp), F32)],
        compiler_params=pltpu.CompilerParams(dimension_semantics=("arbitrary",), vmem_limit_bytes=VMEM_LIMIT),
        name="merge",
    )(x2, act, o, gates, w_conv_out.astype(BF16), b_conv_out.reshape(1, d), w_attn_out.astype(BF16),
      b_attn_out.reshape(1, d), w_out.astype(BF16), g_ffn.reshape(1, d), w_router, b_router.reshape(1, n_exp))


def _moe_kernel(cnt_ref, pos_ref, w_ref, h_ref, wg_ref, bg_ref, wu_ref, bu_ref, wd_ref, bd_ref, x1_ref, gf_ref,
                out_ref, xs_ref, off_ref, ys_ref, *, chunk, tsub, d, n_exp, bm):
    c = pl.program_id(0)
    s = pl.program_id(1)
    nblk = d // LANES
    n_sub = chunk // tsub

    @pl.when((c == 0) & (s == 0))
    def _():
        xs_ref[...] = jnp.zeros_like(xs_ref)

    @pl.when(s == 0)
    def _():
        def offs(j, acc):
            off_ref[j] = acc
            return acc + cnt_ref[c, j]

        lax.fori_loop(0, n_exp, offs, 0)

    @pl.when(s < n_sub)
    def _():
        a0 = s * tsub * TOP_K

        def dispatch(i, carry):
            row = h_ref[pl.ds(pl.multiple_of(i * nblk, nblk), nblk), :]
            for k in range(TOP_K):
                p = pos_ref[a0 + i * TOP_K + k]
                xs_ref[pl.ds(pl.multiple_of(p * nblk, nblk), nblk), :] = row
            return carry

        lax.fori_loop(0, tsub, dispatch, 0, unroll=4)

    @pl.when((s >= n_sub) & (s < n_sub + n_exp))
    def _():
        e = s - n_sub
        cnt = cnt_ref[c, e]
        base = off_ref[e]

        def ffn(slot0, m, n_valid=None):
            row0 = slot0 * nblk
            x = jnp.concatenate([xs_ref[pl.ds(row0 + cb, m, stride=nblk), :].astype(BF16) for cb in range(nblk)],
                                axis=1)
            g = jnp.dot(x, wg_ref[...], preferred_element_type=F32) + bg_ref[...]
            u = jnp.dot(x, wu_ref[...], preferred_element_type=F32) + bu_ref[...]
            g = jnp.minimum(g, SWIGLU_LIMIT)
            u = jnp.clip(u, -SWIGLU_LIMIT, SWIGLU_LIMIT)
            a = (u + 1.0) * (g * jax.nn.sigmoid(SWIGLU_ALPHA * g))
            y = jnp.dot(a.astype(BF16), wd_ref[...], preferred_element_type=F32) + bd_ref[...]
            if n_valid is not None:
                keep = lax.broadcasted_iota(jnp.int32, (m, LANES), 0) < n_valid
            for cb in range(nblk):
                dst = pl.ds(row0 + cb, m, stride=nblk)
                ycb = y[:, cb * LANES:(cb + 1) * LANES]
                xs_ref[dst, :] = ycb if n_valid is None else jnp.where(keep, ycb, xs_ref[dst, :])

        n_big = jnp.maximum((cnt - (bm + 1)) // (2 * bm), 0)

        def big(b, carry):
            ffn(base + b * 2 * bm, 2 * bm)
            return carry

        lax.fori_loop(0, n_big, big, 0)
        rem = cnt - n_big * 2 * bm
        start = base + n_big * 2 * bm
        mid = jnp.where(rem > 2 * bm, bm, 0)

        @pl.when(rem > 2 * bm)
        def _():
            ffn(start, bm)

        rem = rem - mid
        start = start + mid
        for m in range(bm, 2 * bm + 1, BLOCK_STEP):
            lo = 0 if m == bm else m - BLOCK_STEP

            @pl.when((rem > lo) & (rem <= m))
            def _(m=m):
                ffn(start, m, rem)

    @pl.when(s >= n_sub + n_exp)
    def _():
        a0 = (s - n_sub - n_exp) * tsub * TOP_K

        def combine(i, carry):
            acc = jnp.zeros((nblk, LANES), F32)
            for k in range(TOP_K):
                a = a0 + i * TOP_K + k
                acc = acc + w_ref[a] * xs_ref[pl.ds(pl.multiple_of(pos_ref[a] * nblk, nblk), nblk), :]
            ys_ref[pl.ds(pl.multiple_of(i * nblk, nblk), nblk), :] = acc
            return carry

        lax.fori_loop(0, tsub, combine, 0, unroll=4)
        y = jnp.concatenate([ys_ref[pl.ds(cb, tsub, stride=nblk), :] for cb in range(nblk)], axis=1)
        x = x1_ref[...] + y
        ms = jnp.mean(x * x, axis=-1, keepdims=True)
        out_ref[...] = x * lax.rsqrt(ms + EPS) * gf_ref[...]


def _moe(counts, pos_flat, w_flat, hr, wg, bg, wu, bu, wd, bd, x1, g_final, *, chunk, tsub, d, bm):
    n_chunks, n_exp = counts.shape
    nblk = d // LANES
    f = wg.shape[2]
    n_assign = chunk * TOP_K
    n_sub = chunk // tsub
    kern = functools.partial(_moe_kernel, chunk=chunk, tsub=tsub, d=d, n_exp=n_exp, bm=bm)
    smem_blk = lambda: pl.BlockSpec((n_assign,), lambda c, s, cnt: (c,), memory_space=pltpu.SMEM)
    wspec = lambda shp: pl.BlockSpec((None,) + shp, lambda c, s, cnt: (jnp.clip(s - n_sub, 0, n_exp - 1), 0, 0))
    out_tile = lambda c, s, cnt: (c * n_sub + jnp.clip(s - n_sub - n_exp, 0, n_sub - 1), 0)
    grid_spec = pltpu.PrefetchScalarGridSpec(
        num_scalar_prefetch=1,
        grid=(n_chunks, 2 * n_sub + n_exp),
        in_specs=[
            smem_blk(), smem_blk(),
            pl.BlockSpec((tsub * nblk, LANES), lambda c, s, cnt: (c * n_sub + jnp.minimum(s, n_sub - 1), 0)),
            wspec((d, f)), wspec((1, f)), wspec((d, f)), wspec((1, f)), wspec((f, d)), wspec((1, d)),
            pl.BlockSpec((tsub, d), out_tile),
            pl.BlockSpec((1, d), lambda c, s, cnt: (0, 0)),
        ],
        out_specs=pl.BlockSpec((tsub, d), out_tile),
        scratch_shapes=[
            pltpu.VMEM(((n_assign + 2 * bm) * nblk, LANES), F32),
            pltpu.SMEM((n_exp,), jnp.int32),
            pltpu.VMEM((tsub * nblk, LANES), F32),
        ],
    )
    return pl.pallas_call(
        kern,
        grid_spec=grid_spec,
        out_shape=jax.ShapeDtypeStruct((n_chunks * chunk, d), F32),
        compiler_params=pltpu.CompilerParams(dimension_semantics=("arbitrary", "arbitrary"),
                                             vmem_limit_bytes=VMEM_LIMIT_MOE),
        name="moe",
    )(counts, pos_flat, w_flat, hr, wg, bg, wu, bu, wd, bd, x1, g_final.reshape(1, d))


def _pos_kernel(cnt_ref, e_ref, r_ref, pos_ref, *, n_exp):
    cnt = jnp.broadcast_to(cnt_ref[...], (SUBLANES, n_exp))
    hi = (cnt // 256).astype(F32).astype(BF16)
    lo = (cnt % 256).astype(F32).astype(BF16)
    r_i = lax.broadcasted_iota(jnp.int32, (n_exp, n_exp), 0)
    c_i = lax.broadcasted_iota(jnp.int32, (n_exp, n_exp), 1)
    tri = jnp.where(r_i < c_i, 1.0, 0.0).astype(BF16)
    off = (256.0 * jnp.dot(hi, tri, preferred_element_type=F32)
           + jnp.dot(lo, tri, preferred_element_type=F32)).astype(jnp.int32)[0:1]
    e = e_ref[...]
    pos = r_ref[...]
    for j in range(n_exp):
        pos = pos + jnp.where(e == j, off[:, j:j + 1], 0)
    pos_ref[...] = pos


def _pos(counts, e3, r3):
    n_chunks, rows, _ = e3.shape
    n_exp = counts.shape[-1]
    assert rows * LANES // TOP_K < 256 * 256
    blk = pl.BlockSpec((None, rows, LANES), lambda c: (c, 0, 0))
    return pl.pallas_call(
        functools.partial(_pos_kernel, n_exp=n_exp),
        grid=(n_chunks,),
        in_specs=[pl.BlockSpec((None, 1, n_exp), lambda c: (c, 0, 0)), blk, blk],
        out_specs=blk,
        out_shape=jax.ShapeDtypeStruct(e3.shape, jnp.int32),
        compiler_params=pltpu.CompilerParams(dimension_semantics=("arbitrary",), vmem_limit_bytes=VMEM_LIMIT),
        name="pos",
    )(counts, e3, r3)


def _block_sizes(batch, seq):
    n = batch * seq
    tm = min(512, n)
    tt = min(256, seq)
    chunk = min(2048, n)
    return tm, tt, chunk


def kernel(x, g_mix, w_in, b_in, conv_w, conv_b, conv_ln_g, conv_ln_b, w_conv_out, b_conv_out, attn_sinks,
           w_attn_out, b_attn_out, w_out, g_ffn, w_router, b_router, w_gate, b_gate, w_up, b_up, w_down,
           b_down, g_final):
    batch, seq, d = x.shape
    n = batch * seq
    c_conv = conv_w.shape[1]
    n_heads = attn_sinks.shape[0]
    n_q = n_heads * HEAD_DIM
    n_kv = (n_heads // KV_GROUP) * HEAD_DIM
    n_exp = w_router.shape[1]
    assert n_kv == LANES and seq % WINDOW == 0 and d % LANES == 0
    tm, tt, chunk = _block_sizes(batch, seq)
    assert n % tm == 0 and seq % tt == 0 and n % chunk == 0 and chunk % tm == 0

    x2 = x.reshape(n, d)
    u, q, kv, gates, wgb, wub, wdb = _inproj(x2, g_mix, w_in, b_in, w_gate, w_up, w_down,
                                             c_conv=c_conv, n_q=n_q, n_kv=n_kv, tm=tm)
    act = _conv(u.reshape(batch, seq, c_conv), conv_w, conv_b, conv_ln_g, conv_ln_b, tt=tt)
    o = _attn(q.reshape(batch, seq, n_q), kv.reshape(batch, seq, 2 * n_kv), attn_sinks, n_heads)
    x1, hr, ri, rw, counts = _merge(
        x2, act.reshape(n, c_conv), o.reshape(n, n_q), gates, w_conv_out, b_conv_out, w_attn_out, b_attn_out,
        w_out, g_ffn, w_router, b_router, tm=tm, chunk=chunk)
    n_chunks = n // chunk
    rows = chunk * TOP_K // LANES
    pos = _pos(counts, ri[:, :TOP_K].reshape(n_chunks, rows, LANES), ri[:, TOP_K:].reshape(n_chunks, rows, LANES))
    out = _moe(counts.reshape(n_chunks, n_exp), pos.reshape(-1), rw.reshape(-1), hr,
               wgb, b_gate.reshape(n_exp, 1, -1), wub, b_up.reshape(n_exp, 1, -1),
               wdb, b_down.reshape(n_exp, 1, -1), x1, g_final, chunk=chunk, tsub=tm, d=d, bm=128)
    return out.reshape(batch, seq, d)
```

```python
import functools
import math

import jax
import jax.numpy as jnp
from jax import lax
from jax.experimental import pallas as pl
from jax.experimental.pallas import tpu as pltpu

EPS = 1e-5
K_CONV = 31
HEAD_DIM = 64
KV_GROUP = 8
WINDOW = 128
TOP_K = 4
SWIGLU_LIMIT = 7.0
SWIGLU_ALPHA = 1.702

LANES = 128
SUBLANES = 8
HALO = 32
BLOCK_STEP = 32
V7X_VMEM_BYTES = 64 * 1024 * 1024
VMEM_LIMIT = 56 * 1024 * 1024
VMEM_LIMIT_MOE = V7X_VMEM_BYTES - 2 * 1024 * 1024

F32 = jnp.float32
BF16 = jnp.bfloat16


def _const_spec(shape):
    nd = len(shape)
    return pl.BlockSpec(shape, lambda *_: (0,) * nd, pipeline_mode=pl.Buffered(1))


def _inproj_kernel(x_ref, g_ref, w_ref, b_ref, wg_ref, wu_ref, wd_ref,
                   u_ref, q_ref, kv_ref, gate_ref, wgb_ref, wub_ref, wdb_ref, *, c_conv, n_q, n_kv, d_model):
    wgb_ref[...] = wg_ref[...].astype(BF16)
    wub_ref[...] = wu_ref[...].astype(BF16)
    wdb_ref[...] = wd_ref[...].astype(BF16)

    x = x_ref[...]
    ms = jnp.mean(x * x, axis=-1, keepdims=True)
    h = (x * lax.rsqrt(ms + EPS) * g_ref[...]).astype(BF16)
    cw = 512

    def seg(lo, width):
        return jnp.dot(h, w_ref[:, lo:lo + width], preferred_element_type=F32) + b_ref[:, lo:lo + width]

    for c0 in range(0, c_conv, cw):
        a = seg(c0, cw)
        b = seg(c_conv + c0, cw)
        u_ref[:, c0:c0 + cw] = (a * jax.nn.sigmoid(b)).astype(BF16)
    off = 2 * c_conv
    scale = 1.0 / math.sqrt(HEAD_DIM)
    for c0 in range(0, n_q, cw):
        q_ref[:, c0:c0 + cw] = (seg(off + c0, cw) * scale).astype(BF16)
    off += n_q
    kv_ref[...] = seg(off, 2 * n_kv).astype(BF16)
    off += 2 * n_kv
    for c0 in range(0, 2 * d_model, cw):
        gate_ref[:, c0:c0 + cw] = jax.nn.sigmoid(seg(off + c0, cw)).astype(BF16)


def _inproj(x2, g_mix, w_in, b_in, w_gate, w_up, w_down, *, c_conv, n_q, n_kv, tm):
    n, d = x2.shape
    n_in = w_in.shape[1]
    steps = n // tm
    slabs = []
    for w in (w_gate, w_up, w_down):
        rows = w.shape[0] * w.shape[1]
        assert rows % steps == 0 and (rows // steps) % 16 == 0
        slabs.append((rows // steps, w.shape[2]))
    kern = functools.partial(_inproj_kernel, c_conv=c_conv, n_q=n_q, n_kv=n_kv, d_model=d)
    row = lambda i: (i, 0)
    outs = pl.pallas_call(
        kern,
        grid=(steps,),
        in_specs=[
            pl.BlockSpec((tm, d), row),
            _const_spec((1, d)),
            _const_spec((d, n_in)),
            _const_spec((1, n_in)),
        ] + [pl.BlockSpec(sl, row) for sl in slabs],
        out_specs=[
            pl.BlockSpec((tm, c_conv), row),
            pl.BlockSpec((tm, n_q), row),
            pl.BlockSpec((tm, 2 * n_kv), row),
            pl.BlockSpec((tm, 2 * d), row),
        ] + [pl.BlockSpec(sl, row) for sl in slabs],
        out_shape=[
            jax.ShapeDtypeStruct((n, c_conv), BF16),
            jax.ShapeDtypeStruct((n, n_q), BF16),
            jax.ShapeDtypeStruct((n, 2 * n_kv), BF16),
            jax.ShapeDtypeStruct((n, 2 * d), BF16),
        ] + [jax.ShapeDtypeStruct((w.shape[0] * w.shape[1], w.shape[2]), BF16) for w in (w_gate, w_up, w_down)],
        compiler_params=pltpu.CompilerParams(dimension_semantics=("arbitrary",), vmem_limit_bytes=VMEM_LIMIT),
        name="inproj",
    )(x2, g_mix.reshape(1, d), w_in.astype(BF16), b_in.reshape(1, n_in),
      *[w.reshape(w.shape[0] * w.shape[1], w.shape[2]) for w in (w_gate, w_up, w_down)])
    u, q, kv, gates = outs[:4]
    wgb, wub, wdb = [o.reshape(w.shape) for o, w in zip(outs[4:], (w_gate, w_up, w_down))]
    return u, q, kv, gates, wgb, wub, wdb


def _conv_kernel(halo_ref, u_ref, w_ref, cb_ref, lg_ref, lb_ref, act_ref, ext_ref, conv_ref, *, tt, strip):
    i = pl.program_id(1)
    rows = tt + HALO
    halo = jnp.where(i > 0, halo_ref[...].astype(F32), 0.0)
    ext = jnp.concatenate([halo, u_ref[...].astype(F32)], axis=0)
    ext_ref[0] = ext
    for s in range(1, SUBLANES):
        ext_ref[s] = pltpu.roll(ext, rows - s, axis=0)

    rgrp = 8
    rchunk = rgrp * SUBLANES
    for cb in range(ext.shape[1] // LANES):
        lanes = slice(cb * LANES, (cb + 1) * LANES)

        def taps(rc, carry, lanes=lanes):
            base = pl.multiple_of(rc * rchunk, rchunk)
            accs = [jnp.zeros((SUBLANES, LANES), F32) for _ in range(rgrp)]
            for j in range(K_CONV):
                a, s = divmod(HALO - (K_CONV - 1) + j, SUBLANES)
                wv = w_ref[j, :, lanes]
                for g in range(rgrp):
                    accs[g] = accs[g] + ext_ref[s, pl.ds(base + SUBLANES * (a + g), SUBLANES), lanes] * wv
            for g in range(rgrp):
                conv_ref[pl.ds(base + SUBLANES * g, SUBLANES), lanes] = accs[g]
            return carry

        lax.fori_loop(0, tt // rchunk, taps, 0)

    for r in range(tt // strip):
        rows_r = slice(r * strip, (r + 1) * strip)
        v = conv_ref[rows_r, :] + cb_ref[...]
        mu = jnp.mean(v, axis=-1, keepdims=True)
        dv = v - mu
        var = jnp.mean(dv * dv, axis=-1, keepdims=True)
        y = dv * lax.rsqrt(var + EPS) * lg_ref[...] + lb_ref[...]
        act_ref[rows_r, :] = (y * jax.nn.sigmoid(y)).astype(BF16)


def _conv(u3, conv_w, conv_b, ln_g, ln_b, *, tt):
    b, t, c = u3.shape
    hb = tt // HALO
    kern = functools.partial(_conv_kernel, tt=tt, strip=32)
    wpad = jnp.broadcast_to(conv_w[:, None, :], (K_CONV, SUBLANES, c))
    return pl.pallas_call(
        kern,
        grid=(b, t // tt),
        in_specs=[
            pl.BlockSpec((None, HALO, c), lambda bi, i: (bi, jnp.maximum(i * hb - 1, 0), 0)),
            pl.BlockSpec((None, tt, c), lambda bi, i: (bi, i, 0)),
            _const_spec((K_CONV, SUBLANES, c)),
            _const_spec((1, c)),
            _const_spec((1, c)),
            _const_spec((1, c)),
        ],
        out_specs=pl.BlockSpec((None, tt, c), lambda bi, i: (bi, i, 0)),
        out_shape=jax.ShapeDtypeStruct((b, t, c), BF16),
        scratch_shapes=[pltpu.VMEM((SUBLANES, tt + HALO, c), F32), pltpu.VMEM((tt, c), F32)],
        compiler_params=pltpu.CompilerParams(dimension_semantics=("arbitrary", "arbitrary"),
                                             vmem_limit_bytes=VMEM_LIMIT),
        name="conv",
    )(u3, u3, wpad, conv_b.reshape(1, c), ln_g.reshape(1, c), ln_b.reshape(1, c))


def _attn_kernel(q_ref, kvp_ref, kvc_ref, bias_ref, sink_ref, o_ref):
    i = pl.program_id(1)
    qb = WINDOW
    lane = lax.broadcasted_iota(jnp.int32, (2 * qb, LANES), 1)
    lo_half = lane < HEAD_DIM
    kvp = kvp_ref[...].astype(F32)
    kvc = kvc_ref[...].astype(F32)
    kband = jnp.concatenate([kvp[:, :LANES], kvc[:, :LANES]], axis=0)
    vband = jnp.concatenate([kvp[:, LANES:], kvc[:, LANES:]], axis=0)
    kswap = pltpu.roll(kband, HEAD_DIM, axis=1)
    vswap = pltpu.roll(vband, HEAD_DIM, axis=1)
    col = lax.broadcasted_iota(jnp.int32, (qb, 2 * qb), 1)
    first_block_mask = jnp.where((i == 0) & (col < qb), -jnp.inf, 0.0).astype(F32)
    ind_r = lax.broadcasted_iota(jnp.int32, (4 * qb, LANES), 0) < 2 * qb
    ind_c = lax.broadcasted_iota(jnp.int32, (4 * qb, LANES), 1) < HEAD_DIM
    ind = jnp.where(ind_r == ind_c, 1.0, 0.0).astype(BF16)
    lo_lane = lax.broadcasted_iota(jnp.int32, (qb, LANES), 1) < HEAD_DIM

    for h in range(2):
        ksrc_lo, ksrc_hi = (kband, kswap) if h == 0 else (kswap, kband)
        vsrc_lo, vsrc_hi = (vband, vswap) if h == 0 else (vswap, vband)
        kext = jnp.concatenate([jnp.where(lo_half, ksrc_lo, 0.0), jnp.where(lo_half, 0.0, ksrc_hi)],
                               axis=0).astype(BF16)
        vext = jnp.concatenate([jnp.where(lo_half, vsrc_lo, 0.0), jnp.where(lo_half, 0.0, vsrc_hi)],
                               axis=0).astype(BF16)
        vext2 = jnp.concatenate([vext, ind], axis=1)
        for j in range(4):
            cols = slice((4 * h + j) * LANES, (4 * h + j + 1) * LANES)
            s = lax.dot_general(q_ref[:, cols], kext, (((1,), (1,)), ((), ())),
                                preferred_element_type=F32)
            ps, es = [], []
            for half in range(2):
                sh = s[:, half * 2 * qb:(half + 1) * 2 * qb] + bias_ref[h, half, j] + first_block_mask
                sink = sink_ref[8 * h + 2 * j + half]
                m = jnp.maximum(jnp.max(sh, axis=-1, keepdims=True), sink)
                ps.append(jnp.exp(sh - m).astype(BF16))
                es.append(jnp.exp(sink - m))
            r = jnp.dot(jnp.concatenate(ps, axis=1), vext2, preferred_element_type=F32)
            den = r[:, LANES:] + jnp.where(lo_lane, es[0], es[1])
            o_ref[:, cols] = (r[:, :LANES] / den).astype(BF16)


def _alibi_bias(n_heads):
    qb = WINDOW
    slopes = jnp.asarray([2.0 ** (-8.0 * (i + 1) / n_heads) for i in range(n_heads)], dtype=F32)
    qi = jnp.arange(qb)[:, None]
    sj = jnp.arange(2 * qb)[None, :]
    dist = (qi + qb - sj).astype(F32)
    valid = (dist >= 0) & (dist < WINDOW)
    heads = (8 * jnp.arange(2)[:, None, None] + jnp.arange(2)[None, :, None] + 2 * jnp.arange(4)[None, None, :])
    sl = slopes[heads]
    return jnp.where(valid[None, None, None], -sl[..., None, None] * dist, -jnp.inf)


def _attn(q3, kv3, attn_sinks, n_heads):
    b, t, nq = q3.shape
    qb = WINDOW
    bias = _alibi_bias(n_heads)
    return pl.pallas_call(
        _attn_kernel,
        grid=(b, t // qb),
        in_specs=[
            pl.BlockSpec((None, qb, nq), lambda bi, i: (bi, i, 0)),
            pl.BlockSpec((None, qb, kv3.shape[2]), lambda bi, i: (bi, jnp.maximum(i - 1, 0), 0)),
            pl.BlockSpec((None, qb, kv3.shape[2]), lambda bi, i: (bi, i, 0)),
            _const_spec(bias.shape),
            pl.BlockSpec(memory_space=pltpu.SMEM),
        ],
        out_specs=pl.BlockSpec((None, qb, nq), lambda bi, i: (bi, i, 0)),
        out_shape=jax.ShapeDtypeStruct((b, t, nq), BF16),
        compiler_params=pltpu.CompilerParams(dimension_semantics=("arbitrary", "arbitrary"),
                                             vmem_limit_bytes=VMEM_LIMIT),
        name="attn",
    )(q3, kv3, kv3, bias, attn_sinks.astype(F32))


def _merge_kernel(x_ref, act_ref, o_ref, gate_ref, wco_ref, bco_ref, wao_ref, bao_ref, wout_ref,
                  gffn_ref, wrt_ref, brt_ref,
                  x1_ref, hr_ref, e_ref, r_ref, rw_ref, cnt_ref,
                  carry_ref, *, tm, d, n_exp, tiles_per_chunk):
    i = pl.program_id(0)
    yc = jnp.dot(act_ref[...], wco_ref[...], preferred_element_type=F32) + bco_ref[...]
    ya = jnp.dot(o_ref[...], wao_ref[...], preferred_element_type=F32) + bao_ref[...]
    m = gate_ref[:, :d].astype(F32) * yc + gate_ref[:, d:].astype(F32) * ya
    x1 = x_ref[...] + jnp.dot(m.astype(BF16), wout_ref[...], preferred_element_type=F32)
    x1_ref[...] = x1
    ms = jnp.mean(x1 * x1, axis=-1, keepdims=True)
    h2 = x1 * lax.rsqrt(ms + EPS) * gffn_ref[...]
    nblk = d // LANES
    for cb in range(nblk):
        hr_ref[pl.ds(cb, tm, stride=nblk), :] = h2[:, cb * LANES:(cb + 1) * LANES]

    h_hi = h2.astype(BF16)
    h_lo = (h2 - h_hi.astype(F32)).astype(BF16)
    wt = wrt_ref[...]
    wt_hi = wt.astype(BF16)
    wt_lo = (wt - wt_hi.astype(F32)).astype(BF16)
    nt = (((1,), (1,)), ((), ()))
    l = (lax.dot_general(wt_hi, h_hi, nt, preferred_element_type=F32)
         + lax.dot_general(wt_lo, h_hi, nt, preferred_element_type=F32)
         + lax.dot_general(wt_hi, h_lo, nt, preferred_element_type=F32)) + brt_ref[...]

    iota = lax.broadcasted_iota(jnp.int32, (n_exp, tm), 0)
    vals, idxs, sels = [], [], []
    for _ in range(TOP_K):
        mk = jnp.max(l, axis=0, keepdims=True)
        ik = jnp.min(jnp.where(l == mk, iota, n_exp), axis=0, keepdims=True)
        sel = iota == ik
        l = jnp.where(sel, -jnp.inf, l)
        vals.append(mk)
        idxs.append(ik)
        sels.append(sel)
    exs = [jnp.exp(v - vals[0]) for v in vals]
    den = exs[0] + exs[1] + exs[2] + exs[3]

    @pl.when(i % tiles_per_chunk == 0)
    def _():
        carry_ref[...] = jnp.zeros_like(carry_ref)

    member = (sels[0] | sels[1] | sels[2] | sels[3])
    mf = jnp.where(member, 1.0, 0.0).astype(F32)
    r_i = lax.broadcasted_iota(jnp.int32, (tm, tm), 0)
    c_i = lax.broadcasted_iota(jnp.int32, (tm, tm), 1)
    earlier = jnp.where(r_i < c_i, 1.0, 0.0).astype(BF16)
    before = jnp.dot(mf.astype(BF16), earlier, preferred_element_type=F32) + carry_ref[...]
    carry_new = carry_ref[...] + jnp.sum(mf, axis=1, keepdims=True)
    carry_ref[...] = carry_new
    cnt_ref[...] = carry_new.astype(jnp.int32)

    ranks = [jnp.sum(jnp.where(sels[k], before, 0.0), axis=0, keepdims=True).astype(jnp.int32)
             for k in range(TOP_K)]
    e_ref[...] = jnp.concatenate(idxs, axis=0)
    r_ref[...] = jnp.concatenate(ranks, axis=0)
    rw_ref[...] = jnp.concatenate([exs[k] / den for k in range(TOP_K)], axis=0)


def _merge(x2, act, o, gates, w_conv_out, b_conv_out, w_attn_out, b_attn_out, w_out, g_ffn, w_router, b_router,
           *, tm, chunk):
    n, d = x2.shape
    n_exp = w_router.shape[1]
    tiles_per_chunk = chunk // tm
    n_chunks = n // chunk
    nblk = d // LANES
    kern = functools.partial(_merge_kernel, tm=tm, d=d, n_exp=n_exp, tiles_per_chunk=tiles_per_chunk)
    row = lambda i: (i, 0)
    return pl.pallas_call(
        kern,
        grid=(n // tm,),
        in_specs=[
            pl.BlockSpec((tm, d), row),
            pl.BlockSpec((tm, d), row),
            pl.BlockSpec((tm, d), row),
            pl.BlockSpec((tm, 2 * d), row),
            _const_spec((d, d)), _const_spec((1, d)),
            _const_spec((d, d)), _const_spec((1, d)),
            _const_spec((d, d)),
            _const_spec((1, d)),
            _const_spec((n_exp, d)), _const_spec((n_exp, 1)),
        ],
        out_specs=[
            pl.BlockSpec((tm, d), row),
            pl.BlockSpec((tm * nblk, LANES), row),
            pl.BlockSpec((TOP_K, tm), lambda i: (0, i)),
            pl.BlockSpec((TOP_K, tm), lambda i: (0, i)),
            pl.BlockSpec((TOP_K, tm), lambda i: (0, i)),
            pl.BlockSpec((None, n_exp, 1), lambda i: (i // tiles_per_chunk, 0, 0)),
        ],
        out_shape=[
            jax.ShapeDtypeStruct((n, d), F32),
            jax.ShapeDtypeStruct((n * nblk, LANES), F32),
            jax.ShapeDtypeStruct((TOP_K, n), jnp.int32),
            jax.ShapeDtypeStruct((TOP_K, n), jnp.int32),
            jax.ShapeDtypeStruct((TOP_K, n), F32),
            jax.ShapeDtypeStruct((n_chunks, n_exp, 1), jnp.int32),
        ],
        scratch_shapes=[pltpu.VMEM((n_exp, 1), F32)],
        compiler_params=pltpu.CompilerParams(dimension_semantics=("arbitrary",), vmem_limit_bytes=VMEM_LIMIT),
        name="merge",
    )(x2, act, o, gates, w_conv_out.astype(BF16), b_conv_out.reshape(1, d), w_attn_out.astype(BF16),
      b_attn_out.reshape(1, d), w_out.astype(BF16), g_ffn.reshape(1, d), w_router.T, b_router.reshape(n_exp, 1))


def _moe_kernel(cnt_ref, pos_ref, w_ref, h_ref, wg_ref, bg_ref, wu_ref, bu_ref, wd_ref, bd_ref, x1_ref, gf_ref,
                out_ref, xs_ref, off_ref, ys_ref, *, chunk, tsub, d, n_exp, bm):
    c = pl.program_id(0)
    s = pl.program_id(1)
    nblk = d // LANES
    n_sub = chunk // tsub

    @pl.when((c == 0) & (s == 0))
    def _():
        xs_ref[...] = jnp.zeros_like(xs_ref)

    @pl.when(s == 0)
    def _():
        def offs(j, acc):
            off_ref[j] = acc
            return acc + cnt_ref[c, j]

        lax.fori_loop(0, n_exp, offs, 0)

    @pl.when(s < n_sub)
    def _():
        t0 = s * tsub

        def dispatch(i, carry):
            row = h_ref[pl.ds(pl.multiple_of(i * nblk, nblk), nblk), :]
            for k in range(TOP_K):
                p = pos_ref[k * chunk + t0 + i]
                xs_ref[pl.ds(pl.multiple_of(p * nblk, nblk), nblk), :] = row
            return carry

        lax.fori_loop(0, tsub, dispatch, 0, unroll=4)

    @pl.when((s >= n_sub) & (s < n_sub + n_exp))
    def _():
        e = s - n_sub
        cnt = cnt_ref[c, e]
        base = off_ref[e]

        def ffn(slot0, m, n_valid=None):
            row0 = slot0 * nblk
            x = jnp.concatenate([xs_ref[pl.ds(row0 + cb, m, stride=nblk), :].astype(BF16) for cb in range(nblk)],
                                axis=1)
            g = jnp.dot(x, wg_ref[...], preferred_element_type=F32) + bg_ref[...]
            u = jnp.dot(x, wu_ref[...], preferred_element_type=F32) + bu_ref[...]
            g = jnp.minimum(g, SWIGLU_LIMIT)
            u = jnp.clip(u, -SWIGLU_LIMIT, SWIGLU_LIMIT)
            a = (u + 1.0) * (g * jax.nn.sigmoid(SWIGLU_ALPHA * g))
            y = jnp.dot(a.astype(BF16), wd_ref[...], preferred_element_type=F32) + bd_ref[...]
            if n_valid is not None:
                keep = lax.broadcasted_iota(jnp.int32, (m, LANES), 0) < n_valid
            for cb in range(nblk):
                dst = pl.ds(row0 + cb, m, stride=nblk)
                ycb = y[:, cb * LANES:(cb + 1) * LANES]
                xs_ref[dst, :] = ycb if n_valid is None else jnp.where(keep, ycb, xs_ref[dst, :])

        n_big = jnp.maximum((cnt - (bm + 1)) // (2 * bm), 0)

        def big(b, carry):
            ffn(base + b * 2 * bm, 2 * bm)
            return carry

        lax.fori_loop(0, n_big, big, 0)
        rem = cnt - n_big * 2 * bm
        start = base + n_big * 2 * bm
        mid = jnp.where(rem > 2 * bm, bm, 0)

        @pl.when(rem > 2 * bm)
        def _():
            ffn(start, bm)

        rem = rem - mid
        start = start + mid
        for m in range(bm, 2 * bm + 1, BLOCK_STEP):
            lo = 0 if m == bm else m - BLOCK_STEP

            @pl.when((rem > lo) & (rem <= m))
            def _(m=m):
                ffn(start, m, rem)

    @pl.when(s >= n_sub + n_exp)
    def _():
        t0 = (s - n_sub - n_exp) * tsub

        def combine(i, carry):
            acc = jnp.zeros((nblk, LANES), F32)
            for k in range(TOP_K):
                a = k * chunk + t0 + i
                slot = pl.ds(pl.multiple_of(pos_ref[a] * nblk, nblk), nblk)
                acc = acc + w_ref[a] * xs_ref[slot, :]
            ys_ref[pl.ds(pl.multiple_of(i * nblk, nblk), nblk), :] = acc
            return carry

        lax.fori_loop(0, tsub, combine, 0, unroll=4)
        y = jnp.concatenate([ys_ref[pl.ds(cb, tsub, stride=nblk), :] for cb in range(nblk)], axis=1)
        x = x1_ref[...] + y
        ms = jnp.mean(x * x, axis=-1, keepdims=True)
        out_ref[...] = x * lax.rsqrt(ms + EPS) * gf_ref[...]


def _moe(counts, pos, rw, hr, wg, bg, wu, bu, wd, bd, x1, g_final, *, chunk, tsub, d, bm):
    n_chunks, n_exp = counts.shape
    nblk = d // LANES
    f = wg.shape[2]
    n_assign = chunk * TOP_K
    n_sub = chunk // tsub
    kern = functools.partial(_moe_kernel, chunk=chunk, tsub=tsub, d=d, n_exp=n_exp, bm=bm)
    smem_blk = lambda: pl.BlockSpec((n_assign,), lambda c, s, cnt: (c,), memory_space=pltpu.SMEM)
    wspec = lambda shp: pl.BlockSpec((None,) + shp, lambda c, s, cnt: (jnp.clip(s - n_sub, 0, n_exp - 1), 0, 0))
    out_tile = lambda c, s, cnt: (c * n_sub + jnp.clip(s - n_sub - n_exp, 0, n_sub - 1), 0)
    grid_spec = pltpu.PrefetchScalarGridSpec(
        num_scalar_prefetch=1,
        grid=(n_chunks, 2 * n_sub + n_exp),
        in_specs=[
            smem_blk(), smem_blk(),
            pl.BlockSpec((tsub * nblk, LANES), lambda c, s, cnt: (c * n_sub + jnp.minimum(s, n_sub - 1), 0)),
            wspec((d, f)), wspec((1, f)), wspec((d, f)), wspec((1, f)), wspec((f, d)), wspec((1, d)),
            pl.BlockSpec((tsub, d), out_tile),
            pl.BlockSpec((1, d), lambda c, s, cnt: (0, 0)),
        ],
        out_specs=pl.BlockSpec((tsub, d), out_tile),
        scratch_shapes=[
            pltpu.VMEM(((n_assign + 2 * bm) * nblk, LANES), F32),
            pltpu.SMEM((n_exp,), jnp.int32),
            pltpu.VMEM((tsub * nblk, LANES), F32),
        ],
    )
    return pl.pallas_call(
        kern,
        grid_spec=grid_spec,
        out_shape=jax.ShapeDtypeStruct((n_chunks * chunk, d), F32),
        compiler_params=pltpu.CompilerParams(dimension_semantics=("arbitrary", "arbitrary"),
                                             vmem_limit_bytes=VMEM_LIMIT_MOE),
        name="moe",
    )(counts, pos, rw, hr, wg, bg, wu, bu, wd, bd, x1, g_final.reshape(1, d))


def _pos_kernel(cnt_ref, e_ref, r_ref, w_ref, pos_ref, wflat_ref, *, n_exp, chunk):
    cnt = jnp.broadcast_to(cnt_ref[...], (n_exp, LANES))
    hi = (cnt // 256).astype(F32).astype(BF16)
    lo = (cnt % 256).astype(F32).astype(BF16)
    r_i = lax.broadcasted_iota(jnp.int32, (n_exp, n_exp), 0)
    c_i = lax.broadcasted_iota(jnp.int32, (n_exp, n_exp), 1)
    tri = jnp.where(c_i < r_i, 1.0, 0.0).astype(BF16)
    off = (256.0 * jnp.dot(tri, hi, preferred_element_type=F32)
           + jnp.dot(tri, lo, preferred_element_type=F32)).astype(jnp.int32)[:, 0:1]
    e = e_ref[...]
    pos = r_ref[...]
    for j in range(n_exp):
        pos = pos + jnp.where(e == j, off[j:j + 1, :], 0)
    for k in range(TOP_K):
        pos_ref[:, k * chunk:(k + 1) * chunk] = pos[k:k + 1, :]
        wflat_ref[:, k * chunk:(k + 1) * chunk] = w_ref[k:k + 1, :]


def _pos(counts, e, r, rw, *, chunk):
    n_chunks, n_exp, _ = counts.shape
    assert chunk < 256 * 256 and chunk % LANES == 0
    blk = pl.BlockSpec((TOP_K, chunk), lambda c: (0, c))
    flat = pl.BlockSpec((None, 1, TOP_K * chunk), lambda c: (c, 0, 0))
    pos, wflat = pl.pallas_call(
        functools.partial(_pos_kernel, n_exp=n_exp, chunk=chunk),
        grid=(n_chunks,),
        in_specs=[pl.BlockSpec((None, n_exp, 1), lambda c: (c, 0, 0)), blk, blk, blk],
        out_specs=[flat, flat],
        out_shape=[jax.ShapeDtypeStruct((n_chunks, 1, TOP_K * chunk), jnp.int32),
                   jax.ShapeDtypeStruct((n_chunks, 1, TOP_K * chunk), F32)],
        compiler_params=pltpu.CompilerParams(dimension_semantics=("arbitrary",), vmem_limit_bytes=VMEM_LIMIT),
        name="pos",
    )(counts, e, r, rw)
    return pos.reshape(-1), wflat.reshape(-1)


def _block_sizes(batch, seq):
    n = batch * seq
    tm = min(512, n)
    tt = min(256, seq)
    chunk = min(2048, n)
    return tm, tt, chunk


def kernel(x, g_mix, w_in, b_in, conv_w, conv_b, conv_ln_g, conv_ln_b, w_conv_out, b_conv_out, attn_sinks,
           w_attn_out, b_attn_out, w_out, g_ffn, w_router, b_router, w_gate, b_gate, w_up, b_up, w_down,
           b_down, g_final):
    batch, seq, d = x.shape
    n = batch * seq
    c_conv = conv_w.shape[1]
    n_heads = attn_sinks.shape[0]
    n_q = n_heads * HEAD_DIM
    n_kv = (n_heads // KV_GROUP) * HEAD_DIM
    n_exp = w_router.shape[1]
    assert n_kv == LANES and seq % WINDOW == 0 and d % LANES == 0
    tm, tt, chunk = _block_sizes(batch, seq)
    assert n % tm == 0 and seq % tt == 0 and n % chunk == 0 and chunk % tm == 0

    x2 = x.reshape(n, d)
    u, q, kv, gates, wgb, wub, wdb = _inproj(x2, g_mix, w_in, b_in, w_gate, w_up, w_down,
                                             c_conv=c_conv, n_q=n_q, n_kv=n_kv, tm=tm)
    act = _conv(u.reshape(batch, seq, c_conv), conv_w, conv_b, conv_ln_g, conv_ln_b, tt=tt)
    o = _attn(q.reshape(batch, seq, n_q), kv.reshape(batch, seq, 2 * n_kv), attn_sinks, n_heads)
    x1, hr, e, r, rw, counts = _merge(
        x2, act.reshape(n, c_conv), o.reshape(n, n_q), gates, w_conv_out, b_conv_out, w_attn_out, b_attn_out,
        w_out, g_ffn, w_router, b_router, tm=tm, chunk=chunk)
    pos, wflat = _pos(counts, e, r, rw, chunk=chunk)
    out = _moe(counts.reshape(n // chunk, n_exp), pos, wflat, hr,
               wgb, b_gate.reshape(n_exp, 1, -1), wub, b_up.reshape(n_exp, 1, -1),
               wdb, b_down.reshape(n_exp, 1, -1), x1, g_final, chunk=chunk, tsub=tm, d=d, bm=128)
    return out.reshape(batch, seq, d)
```

```python
import functools
import math

import jax
import jax.numpy as jnp
from jax import lax
from jax.experimental import pallas as pl
from jax.experimental.pallas import tpu as pltpu

EPS = 1e-5
K_CONV = 31
HEAD_DIM = 64
KV_GROUP = 8
WINDOW = 128
TOP_K = 4
SWIGLU_LIMIT = 7.0
SWIGLU_ALPHA = 1.702

LANES = 128
SUBLANES = 8
HALO = 32
BLOCK_STEP = 32
V7X_VMEM_BYTES = 64 * 1024 * 1024
VMEM_LIMIT = 56 * 1024 * 1024
VMEM_LIMIT_MOE = V7X_VMEM_BYTES - 2 * 1024 * 1024

F32 = jnp.float32
BF16 = jnp.bfloat16


def _const_spec(shape):
    nd = len(shape)
    return pl.BlockSpec(shape, lambda *_: (0,) * nd, pipeline_mode=pl.Buffered(1))


def _inproj_kernel(x_ref, g_ref, w_ref, b_ref, wg_ref, wu_ref, wd_ref,
                   u_ref, q_ref, kv_ref, gate_ref, wgb_ref, wub_ref, wdb_ref, *, c_conv, n_q, n_kv, d_model):
    wgb_ref[...] = wg_ref[...].astype(BF16)
    wub_ref[...] = wu_ref[...].astype(BF16)
    wdb_ref[...] = wd_ref[...].astype(BF16)

    x = x_ref[...]
    ms = jnp.mean(x * x, axis=-1, keepdims=True)
    h = (x * lax.rsqrt(ms + EPS) * g_ref[...]).astype(BF16)
    cw = 512

    def seg(lo, width):
        return jnp.dot(h, w_ref[:, lo:lo + width], preferred_element_type=F32) + b_ref[:, lo:lo + width]

    for c0 in range(0, c_conv, cw):
        a = seg(c0, cw)
        b = seg(c_conv + c0, cw)
        u_ref[:, c0:c0 + cw] = (a * jax.nn.sigmoid(b)).astype(BF16)
    off = 2 * c_conv
    scale = 1.0 / math.sqrt(HEAD_DIM)
    for c0 in range(0, n_q, cw):
        q_ref[:, c0:c0 + cw] = (seg(off + c0, cw) * scale).astype(BF16)
    off += n_q
    kv_ref[...] = seg(off, 2 * n_kv).astype(BF16)
    off += 2 * n_kv
    for c0 in range(0, 2 * d_model, cw):
        gate_ref[:, c0:c0 + cw] = jax.nn.sigmoid(seg(off + c0, cw)).astype(BF16)


def _inproj(x2, g_mix, w_in, b_in, w_gate, w_up, w_down, *, c_conv, n_q, n_kv, tm):
    n, d = x2.shape
    n_in = w_in.shape[1]
    steps = n // tm
    slabs = []
    for w in (w_gate, w_up, w_down):
        rows = w.shape[0] * w.shape[1]
        assert rows % steps == 0 and (rows // steps) % 16 == 0
        slabs.append((rows // steps, w.shape[2]))
    kern = functools.partial(_inproj_kernel, c_conv=c_conv, n_q=n_q, n_kv=n_kv, d_model=d)
    row = lambda i: (i, 0)
    outs = pl.pallas_call(
        kern,
        grid=(steps,),
        in_specs=[
            pl.BlockSpec((tm, d), row),
            _const_spec((1, d)),
            _const_spec((d, n_in)),
            _const_spec((1, n_in)),
        ] + [pl.BlockSpec(sl, row) for sl in slabs],
        out_specs=[
            pl.BlockSpec((tm, c_conv), row),
            pl.BlockSpec((tm, n_q), row),
            pl.BlockSpec((tm, 2 * n_kv), row),
            pl.BlockSpec((tm, 2 * d), row),
        ] + [pl.BlockSpec(sl, row) for sl in slabs],
        out_shape=[
            jax.ShapeDtypeStruct((n, c_conv), BF16),
            jax.ShapeDtypeStruct((n, n_q), BF16),
            jax.ShapeDtypeStruct((n, 2 * n_kv), BF16),
            jax.ShapeDtypeStruct((n, 2 * d), BF16),
        ] + [jax.ShapeDtypeStruct((w.shape[0] * w.shape[1], w.shape[2]), BF16) for w in (w_gate, w_up, w_down)],
        compiler_params=pltpu.CompilerParams(dimension_semantics=("arbitrary",), vmem_limit_bytes=VMEM_LIMIT),
        name="inproj",
    )(x2, g_mix.reshape(1, d), w_in.astype(BF16), b_in.reshape(1, n_in),
      *[w.reshape(w.shape[0] * w.shape[1], w.shape[2]) for w in (w_gate, w_up, w_down)])
    u, q, kv, gates = outs[:4]
    wgb, wub, wdb = [o.reshape(w.shape) for o, w in zip(outs[4:], (w_gate, w_up, w_down))]
    return u, q, kv, gates, wgb, wub, wdb


def _conv_kernel(halo_ref, u_ref, w_ref, cb_ref, lg_ref, lb_ref, act_ref, ext_ref, conv_ref, *, tt, strip):
    i = pl.program_id(1)
    rows = tt + HALO
    halo = jnp.where(i > 0, halo_ref[...].astype(F32), 0.0)
    ext = jnp.concatenate([halo, u_ref[...].astype(F32)], axis=0)
    ext_ref[0] = ext
    for s in range(1, SUBLANES):
        ext_ref[s] = pltpu.roll(ext, rows - s, axis=0)

    rgrp = 16
    rchunk = rgrp * SUBLANES
    for cb in range(ext.shape[1] // LANES):
        lanes = slice(cb * LANES, (cb + 1) * LANES)

        def taps(rc, carry, lanes=lanes):
            base = pl.multiple_of(rc * rchunk, rchunk)
            accs = [jnp.zeros((SUBLANES, LANES), F32) for _ in range(rgrp)]
            for j in range(K_CONV):
                a, s = divmod(HALO - (K_CONV - 1) + j, SUBLANES)
                wv = w_ref[j, :, lanes]
                for g in range(rgrp):
                    accs[g] = accs[g] + ext_ref[s, pl.ds(base + SUBLANES * (a + g), SUBLANES), lanes] * wv
            for g in range(rgrp):
                conv_ref[pl.ds(base + SUBLANES * g, SUBLANES), lanes] = accs[g]
            return carry

        lax.fori_loop(0, tt // rchunk, taps, 0)

    for r in range(tt // strip):
        rows_r = slice(r * strip, (r + 1) * strip)
        v = conv_ref[rows_r, :] + cb_ref[...]
        mu = jnp.mean(v, axis=-1, keepdims=True)
        dv = v - mu
        var = jnp.mean(dv * dv, axis=-1, keepdims=True)
        y = dv * lax.rsqrt(var + EPS) * lg_ref[...] + lb_ref[...]
        act_ref[rows_r, :] = (y * jax.nn.sigmoid(y)).astype(BF16)


def _conv(u3, conv_w, conv_b, ln_g, ln_b, *, tt):
    b, t, c = u3.shape
    hb = tt // HALO
    kern = functools.partial(_conv_kernel, tt=tt, strip=32)
    wpad = jnp.broadcast_to(conv_w[:, None, :], (K_CONV, SUBLANES, c))
    return pl.pallas_call(
        kern,
        grid=(b, t // tt),
        in_specs=[
            pl.BlockSpec((None, HALO, c), lambda bi, i: (bi, jnp.maximum(i * hb - 1, 0), 0)),
            pl.BlockSpec((None, tt, c), lambda bi, i: (bi, i, 0)),
            _const_spec((K_CONV, SUBLANES, c)),
            _const_spec((1, c)),
            _const_spec((1, c)),
            _const_spec((1, c)),
        ],
        out_specs=pl.BlockSpec((None, tt, c), lambda bi, i: (bi, i, 0)),
        out_shape=jax.ShapeDtypeStruct((b, t, c), BF16),
        scratch_shapes=[pltpu.VMEM((SUBLANES, tt + HALO, c), F32), pltpu.VMEM((tt, c), F32)],
        compiler_params=pltpu.CompilerParams(dimension_semantics=("arbitrary", "arbitrary"),
                                             vmem_limit_bytes=VMEM_LIMIT),
        name="conv",
    )(u3, u3, wpad, conv_b.reshape(1, c), ln_g.reshape(1, c), ln_b.reshape(1, c))


def _attn_kernel(q_ref, kvp_ref, kvc_ref, bias_ref, sink_ref, o_ref, *, nsub):
    i = pl.program_id(1)
    qb = WINDOW
    lane = lax.broadcasted_iota(jnp.int32, (2 * qb, LANES), 1)
    lo_half = lane < HEAD_DIM
    col = lax.broadcasted_iota(jnp.int32, (qb, 2 * qb), 1)
    ind_r = lax.broadcasted_iota(jnp.int32, (4 * qb, LANES), 0) < 2 * qb
    ind_c = lax.broadcasted_iota(jnp.int32, (4 * qb, LANES), 1) < HEAD_DIM
    ind = jnp.where(ind_r == ind_c, 1.0, 0.0).astype(BF16)
    lo_lane = lax.broadcasted_iota(jnp.int32, (qb, LANES), 1) < HEAD_DIM
    kv_blocks = [kvp_ref[...].astype(F32)] + [kvc_ref[b * qb:(b + 1) * qb, :].astype(F32) for b in range(nsub)]

    for b in range(nsub):
        rows = slice(b * qb, (b + 1) * qb)
        kvp, kvc = kv_blocks[b], kv_blocks[b + 1]
        kband = jnp.concatenate([kvp[:, :LANES], kvc[:, :LANES]], axis=0)
        vband = jnp.concatenate([kvp[:, LANES:], kvc[:, LANES:]], axis=0)
        kswap = pltpu.roll(kband, HEAD_DIM, axis=1)
        vswap = pltpu.roll(vband, HEAD_DIM, axis=1)
        if b == 0:
            first_block_mask = jnp.where((i == 0) & (col < qb), -jnp.inf, 0.0).astype(F32)
        for h in range(2):
            ksrc_lo, ksrc_hi = (kband, kswap) if h == 0 else (kswap, kband)
            vsrc_lo, vsrc_hi = (vband, vswap) if h == 0 else (vswap, vband)
            kext = jnp.concatenate([jnp.where(lo_half, ksrc_lo, 0.0), jnp.where(lo_half, 0.0, ksrc_hi)],
                                   axis=0).astype(BF16)
            vext = jnp.concatenate([jnp.where(lo_half, vsrc_lo, 0.0), jnp.where(lo_half, 0.0, vsrc_hi)],
                                   axis=0).astype(BF16)
            vext2 = jnp.concatenate([vext, ind], axis=1)
            for j in range(4):
                cols = slice((4 * h + j) * LANES, (4 * h + j + 1) * LANES)
                s = lax.dot_general(q_ref[rows, cols], kext, (((1,), (1,)), ((), ())),
                                    preferred_element_type=F32)
                ps, es = [], []
                for half in range(2):
                    sh = s[:, half * 2 * qb:(half + 1) * 2 * qb] + bias_ref[h, half, j]
                    if b == 0:
                        sh = sh + first_block_mask
                    sink = sink_ref[8 * h + 2 * j + half]
                    m = jnp.maximum(jnp.max(sh, axis=-1, keepdims=True), sink)
                    ps.append(jnp.exp(sh - m).astype(BF16))
                    es.append(jnp.exp(sink - m))
                r = jnp.dot(jnp.concatenate(ps, axis=1), vext2, preferred_element_type=F32)
                den = r[:, LANES:] + jnp.where(lo_lane, es[0], es[1])
                o_ref[rows, cols] = (r[:, :LANES] / den).astype(BF16)


def _alibi_bias(n_heads):
    qb = WINDOW
    slopes = jnp.asarray([2.0 ** (-8.0 * (i + 1) / n_heads) for i in range(n_heads)], dtype=F32)
    qi = jnp.arange(qb)[:, None]
    sj = jnp.arange(2 * qb)[None, :]
    dist = (qi + qb - sj).astype(F32)
    valid = (dist >= 0) & (dist < WINDOW)
    heads = (8 * jnp.arange(2)[:, None, None] + jnp.arange(2)[None, :, None] + 2 * jnp.arange(4)[None, None, :])
    sl = slopes[heads]
    return jnp.where(valid[None, None, None], -sl[..., None, None] * dist, -jnp.inf)


def _attn(q3, kv3, attn_sinks, n_heads, *, nsub):
    b, t, nq = q3.shape
    qb = WINDOW
    ta = nsub * qb
    bias = _alibi_bias(n_heads)
    return pl.pallas_call(
        functools.partial(_attn_kernel, nsub=nsub),
        grid=(b, t // ta),
        in_specs=[
            pl.BlockSpec((None, ta, nq), lambda bi, i: (bi, i, 0)),
            pl.BlockSpec((None, qb, kv3.shape[2]), lambda bi, i: (bi, jnp.maximum(i * nsub - 1, 0), 0)),
            pl.BlockSpec((None, ta, kv3.shape[2]), lambda bi, i: (bi, i, 0)),
            _const_spec(bias.shape),
            pl.BlockSpec(memory_space=pltpu.SMEM),
        ],
        out_specs=pl.BlockSpec((None, ta, nq), lambda bi, i: (bi, i, 0)),
        out_shape=jax.ShapeDtypeStruct((b, t, nq), BF16),
        compiler_params=pltpu.CompilerParams(dimension_semantics=("arbitrary", "arbitrary"),
                                             vmem_limit_bytes=VMEM_LIMIT),
        name="attn",
    )(q3, kv3, kv3, bias, attn_sinks.astype(F32))


def _merge_kernel(x_ref, act_ref, o_ref, gate_ref, wco_ref, bco_ref, wao_ref, bao_ref, wout_ref,
                  gffn_ref, wrt_ref, brt_ref,
                  x1_ref, hr_ref, e_ref, r_ref, rw_ref, cnt_ref,
                  carry_ref, *, tm, d, n_exp, tiles_per_chunk):
    i = pl.program_id(0)
    yc = jnp.dot(act_ref[...], wco_ref[...], preferred_element_type=F32) + bco_ref[...]
    ya = jnp.dot(o_ref[...], wao_ref[...], preferred_element_type=F32) + bao_ref[...]
    m = gate_ref[:, :d].astype(F32) * yc + gate_ref[:, d:].astype(F32) * ya
    x1 = x_ref[...] + jnp.dot(m.astype(BF16), wout_ref[...], preferred_element_type=F32)
    x1_ref[...] = x1
    ms = jnp.mean(x1 * x1, axis=-1, keepdims=True)
    h2 = x1 * lax.rsqrt(ms + EPS) * gffn_ref[...]
    nblk = d // LANES
    for cb in range(nblk):
        hr_ref[pl.ds(cb, tm, stride=nblk), :] = h2[:, cb * LANES:(cb + 1) * LANES]

    h_hi = h2.astype(BF16)
    h_lo = (h2 - h_hi.astype(F32)).astype(BF16)
    wt = wrt_ref[...]
    wt_hi = wt.astype(BF16)
    wt_lo = (wt - wt_hi.astype(F32)).astype(BF16)
    nt = (((1,), (1,)), ((), ()))
    l = (lax.dot_general(wt_hi, h_hi, nt, preferred_element_type=F32)
         + lax.dot_general(wt_lo, h_hi, nt, preferred_element_type=F32)
         + lax.dot_general(wt_hi, h_lo, nt, preferred_element_type=F32)) + brt_ref[...]

    iota = lax.broadcasted_iota(jnp.int32, (n_exp, tm), 0)
    vals, idxs, sels = [], [], []
    for _ in range(TOP_K):
        mk = jnp.max(l, axis=0, keepdims=True)
        ik = jnp.min(jnp.where(l == mk, iota, n_exp), axis=0, keepdims=True)
        sel = iota == ik
        l = jnp.where(sel, -jnp.inf, l)
        vals.append(mk)
        idxs.append(ik)
        sels.append(sel)
    exs = [jnp.exp(v - vals[0]) for v in vals]
    den = exs[0] + exs[1] + exs[2] + exs[3]

    @pl.when(i % tiles_per_chunk == 0)
    def _():
        carry_ref[...] = jnp.zeros_like(carry_ref)

    member = (sels[0] | sels[1] | sels[2] | sels[3])
    mf = jnp.where(member, 1.0, 0.0).astype(F32)
    r_i = lax.broadcasted_iota(jnp.int32, (tm, tm), 0)
    c_i = lax.broadcasted_iota(jnp.int32, (tm, tm), 1)
    earlier = jnp.where(r_i < c_i, 1.0, 0.0).astype(BF16)
    before = jnp.dot(mf.astype(BF16), earlier, preferred_element_type=F32) + carry_ref[...]
    carry_new = carry_ref[...] + jnp.sum(mf, axis=1, keepdims=True)
    carry_ref[...] = carry_new
    cnt_ref[...] = carry_new.astype(jnp.int32)

    ranks = [jnp.sum(jnp.where(sels[k], before, 0.0), axis=0, keepdims=True).astype(jnp.int32)
             for k in range(TOP_K)]
    e_ref[...] = jnp.concatenate(idxs, axis=0)
    r_ref[...] = jnp.concatenate(ranks, axis=0)
    rw_ref[...] = jnp.concatenate([exs[k] / den for k in range(TOP_K)], axis=0)


def _merge(x2, act, o, gates, w_conv_out, b_conv_out, w_attn_out, b_attn_out, w_out, g_ffn, w_router, b_router,
           *, tm, chunk):
    n, d = x2.shape
    n_exp = w_router.shape[1]
    tiles_per_chunk = chunk // tm
    n_chunks = n // chunk
    nblk = d // LANES
    kern = functools.partial(_merge_kernel, tm=tm, d=d, n_exp=n_exp, tiles_per_chunk=tiles_per_chunk)
    row = lambda i: (i, 0)
    return pl.pallas_call(
        kern,
        grid=(n // tm,),
        in_specs=[
            pl.BlockSpec((tm, d), row),
            pl.BlockSpec((tm, d), row),
            pl.BlockSpec((tm, d), row),
            pl.BlockSpec((tm, 2 * d), row),
            _const_spec((d, d)), _const_spec((1, d)),
            _const_spec((d, d)), _const_spec((1, d)),
            _const_spec((d, d)),
            _const_spec((1, d)),
            _const_spec((n_exp, d)), _const_spec((n_exp, 1)),
        ],
        out_specs=[
            pl.BlockSpec((tm, d), row),
            pl.BlockSpec((tm * nblk, LANES), row),
            pl.BlockSpec((TOP_K, tm), lambda i: (0, i)),
            pl.BlockSpec((TOP_K, tm), lambda i: (0, i)),
            pl.BlockSpec((TOP_K, tm), lambda i: (0, i)),
            pl.BlockSpec((None, n_exp, 1), lambda i: (i // tiles_per_chunk, 0, 0)),
        ],
        out_shape=[
            jax.ShapeDtypeStruct((n, d), F32),
            jax.ShapeDtypeStruct((n * nblk, LANES), F32),
            jax.ShapeDtypeStruct((TOP_K, n), jnp.int32),
            jax.ShapeDtypeStruct((TOP_K, n), jnp.int32),
            jax.ShapeDtypeStruct((TOP_K, n), F32),
            jax.ShapeDtypeStruct((n_chunks, n_exp, 1), jnp.int32),
        ],
        scratch_shapes=[pltpu.VMEM((n_exp, 1), F32)],
        compiler_params=pltpu.CompilerParams(dimension_semantics=("arbitrary",), vmem_limit_bytes=VMEM_LIMIT),
        name="merge",
    )(x2, act, o, gates, w_conv_out.astype(BF16), b_conv_out.reshape(1, d), w_attn_out.astype(BF16),
      b_attn_out.reshape(1, d), w_out.astype(BF16), g_ffn.reshape(1, d), w_router.T, b_router.reshape(n_exp, 1))


def _moe_kernel(cnt_ref, pos_ref, w_ref, h_ref, wg_ref, bg_ref, wu_ref, bu_ref, wd_ref, bd_ref, x1_ref, gf_ref,
                out_ref, xs_ref, off_ref, ys_ref, *, chunk, tsub, d, n_exp, bm):
    c = pl.program_id(0)
    s = pl.program_id(1)
    nblk = d // LANES
    n_sub = chunk // tsub

    @pl.when((c == 0) & (s == 0))
    def _():
        xs_ref[...] = jnp.zeros_like(xs_ref)

    @pl.when(s == 0)
    def _():
        def offs(j, acc):
            off_ref[j] = acc
            return acc + cnt_ref[c, j]

        lax.fori_loop(0, n_exp, offs, 0)

    @pl.when(s < n_sub)
    def _():
        t0 = s * tsub

        def dispatch(i, carry):
            row = h_ref[pl.ds(pl.multiple_of(i * nblk, nblk), nblk), :]
            for k in range(TOP_K):
                p = pos_ref[k * chunk + t0 + i]
                xs_ref[pl.ds(pl.multiple_of(p * nblk, nblk), nblk), :] = row
            return carry

        lax.fori_loop(0, tsub, dispatch, 0, unroll=4)

    @pl.when((s >= n_sub) & (s < n_sub + n_exp))
    def _():
        e = s - n_sub
        cnt = cnt_ref[c, e]
        base = off_ref[e]

        def ffn(slot0, m, n_valid=None):
            row0 = slot0 * nblk
            x = jnp.concatenate([xs_ref[pl.ds(row0 + cb, m, stride=nblk), :].astype(BF16) for cb in range(nblk)],
                                axis=1)
            g = jnp.dot(x, wg_ref[...], preferred_element_type=F32) + bg_ref[...]
            u = jnp.dot(x, wu_ref[...], preferred_element_type=F32) + bu_ref[...]
            g = jnp.minimum(g, SWIGLU_LIMIT)
            u = jnp.clip(u, -SWIGLU_LIMIT, SWIGLU_LIMIT)
            a = (u + 1.0) * (g * jax.nn.sigmoid(SWIGLU_ALPHA * g))
            y = jnp.dot(a.astype(BF16), wd_ref[...], preferred_element_type=F32) + bd_ref[...]
            if n_valid is not None:
                keep = lax.broadcasted_iota(jnp.int32, (m, LANES), 0) < n_valid
            for cb in range(nblk):
                dst = pl.ds(row0 + cb, m, stride=nblk)
                ycb = y[:, cb * LANES:(cb + 1) * LANES]
                xs_ref[dst, :] = ycb if n_valid is None else jnp.where(keep, ycb, xs_ref[dst, :])

        n_big = jnp.maximum((cnt - (bm + 1)) // (2 * bm), 0)

        def big(b, carry):
            ffn(base + b * 2 * bm, 2 * bm)
            return carry

        lax.fori_loop(0, n_big, big, 0)
        rem = cnt - n_big * 2 * bm
        start = base + n_big * 2 * bm
        mid = jnp.where(rem > 2 * bm, bm, 0)

        @pl.when(rem > 2 * bm)
        def _():
            ffn(start, bm)

        rem = rem - mid
        start = start + mid
        for m in range(bm, 2 * bm + 1, BLOCK_STEP):
            lo = 0 if m == bm else m - BLOCK_STEP

            @pl.when((rem > lo) & (rem <= m))
            def _(m=m):
                ffn(start, m, rem)

    @pl.when(s >= n_sub + n_exp)
    def _():
        t0 = (s - n_sub - n_exp) * tsub

        def combine(i, carry):
            acc = jnp.zeros((nblk, LANES), F32)
            for k in range(TOP_K):
                a = k * chunk + t0 + i
                slot = pl.ds(pl.multiple_of(pos_ref[a] * nblk, nblk), nblk)
                acc = acc + w_ref[a] * xs_ref[slot, :]
            ys_ref[pl.ds(pl.multiple_of(i * nblk, nblk), nblk), :] = acc
            return carry

        lax.fori_loop(0, tsub, combine, 0, unroll=4)
        y = jnp.concatenate([ys_ref[pl.ds(cb, tsub, stride=nblk), :] for cb in range(nblk)], axis=1)
        x = x1_ref[...] + y
        ms = jnp.mean(x * x, axis=-1, keepdims=True)
        out_ref[...] = x * lax.rsqrt(ms + EPS) * gf_ref[...]


def _moe(counts, pos, rw, hr, wg, bg, wu, bu, wd, bd, x1, g_final, *, chunk, tsub, d, bm):
    n_chunks, n_exp = counts.shape
    nblk = d // LANES
    f = wg.shape[2]
    n_assign = chunk * TOP_K
    n_sub = chunk // tsub
    kern = functools.partial(_moe_kernel, chunk=chunk, tsub=tsub, d=d, n_exp=n_exp, bm=bm)
    smem_blk = lambda: pl.BlockSpec((n_assign,), lambda c, s, cnt: (c,), memory_space=pltpu.SMEM)
    wspec = lambda shp: pl.BlockSpec((None,) + shp, lambda c, s, cnt: (jnp.clip(s - n_sub, 0, n_exp - 1), 0, 0))
    out_tile = lambda c, s, cnt: (c * n_sub + jnp.clip(s - n_sub - n_exp, 0, n_sub - 1), 0)
    grid_spec = pltpu.PrefetchScalarGridSpec(
        num_scalar_prefetch=1,
        grid=(n_chunks, 2 * n_sub + n_exp),
        in_specs=[
            smem_blk(), smem_blk(),
            pl.BlockSpec((tsub * nblk, LANES), lambda c, s, cnt: (c * n_sub + jnp.minimum(s, n_sub - 1), 0)),
            wspec((d, f)), wspec((1, f)), wspec((d, f)), wspec((1, f)), wspec((f, d)), wspec((1, d)),
            pl.BlockSpec((tsub, d), out_tile),
            pl.BlockSpec((1, d), lambda c, s, cnt: (0, 0)),
        ],
        out_specs=pl.BlockSpec((tsub, d), out_tile),
        scratch_shapes=[
            pltpu.VMEM(((n_assign + 2 * bm) * nblk, LANES), F32),
            pltpu.SMEM((n_exp,), jnp.int32),
            pltpu.VMEM((tsub * nblk, LANES), F32),
        ],
    )
    return pl.pallas_call(
        kern,
        grid_spec=grid_spec,
        out_shape=jax.ShapeDtypeStruct((n_chunks * chunk, d), F32),
        compiler_params=pltpu.CompilerParams(dimension_semantics=("arbitrary", "arbitrary"),
                                             vmem_limit_bytes=VMEM_LIMIT_MOE),
        name="moe",
    )(counts, pos, rw, hr, wg, bg, wu, bu, wd, bd, x1, g_final.reshape(1, d))


def _pos_kernel(cnt_ref, e_ref, r_ref, w_ref, pos_ref, wflat_ref, *, n_exp, chunk):
    cnt = jnp.broadcast_to(cnt_ref[...], (n_exp, LANES))
    hi = (cnt // 256).astype(F32).astype(BF16)
    lo = (cnt % 256).astype(F32).astype(BF16)
    r_i = lax.broadcasted_iota(jnp.int32, (n_exp, n_exp), 0)
    c_i = lax.broadcasted_iota(jnp.int32, (n_exp, n_exp), 1)
    tri = jnp.where(c_i < r_i, 1.0, 0.0).astype(BF16)
    off = (256.0 * jnp.dot(tri, hi, preferred_element_type=F32)
           + jnp.dot(tri, lo, preferred_element_type=F32)).astype(jnp.int32)[:, 0:1]
    e = e_ref[...]
    pos = r_ref[...]
    for j in range(n_exp):
        pos = pos + jnp.where(e == j, off[j:j + 1, :], 0)
    for k in range(TOP_K):
        pos_ref[:, k * chunk:(k + 1) * chunk] = pos[k:k + 1, :]
        wflat_ref[:, k * chunk:(k + 1) * chunk] = w_ref[k:k + 1, :]


def _pos(counts, e, r, rw, *, chunk):
    n_chunks, n_exp, _ = counts.shape
    assert chunk < 256 * 256 and chunk % LANES == 0
    blk = pl.BlockSpec((TOP_K, chunk), lambda c: (0, c))
    flat = pl.BlockSpec((None, 1, TOP_K * chunk), lambda c: (c, 0, 0))
    pos, wflat = pl.pallas_call(
        functools.partial(_pos_kernel, n_exp=n_exp, chunk=chunk),
        grid=(n_chunks,),
        in_specs=[pl.BlockSpec((None, n_exp, 1), lambda c: (c, 0, 0)), blk, blk, blk],
        out_specs=[flat, flat],
        out_shape=[jax.ShapeDtypeStruct((n_chunks, 1, TOP_K * chunk), jnp.int32),
                   jax.ShapeDtypeStruct((n_chunks, 1, TOP_K * chunk), F32)],
        compiler_params=pltpu.CompilerParams(dimension_semantics=("arbitrary",), vmem_limit_bytes=VMEM_LIMIT),
        name="pos",
    )(counts, e, r, rw)
    return pos.reshape(-1), wflat.reshape(-1)


def _block_sizes(batch, seq):
    n = batch * seq
    tm = min(512, n)
    tt = min(512, seq)
    chunk = min(2048, n)
    return tm, tt, chunk


def kernel(x, g_mix, w_in, b_in, conv_w, conv_b, conv_ln_g, conv_ln_b, w_conv_out, b_conv_out, attn_sinks,
           w_attn_out, b_attn_out, w_out, g_ffn, w_router, b_router, w_gate, b_gate, w_up, b_up, w_down,
           b_down, g_final):
    batch, seq, d = x.shape
    n = batch * seq
    c_conv = conv_w.shape[1]
    n_heads = attn_sinks.shape[0]
    n_q = n_heads * HEAD_DIM
    n_kv = (n_heads // KV_GROUP) * HEAD_DIM
    n_exp = w_router.shape[1]
    assert n_kv == LANES and seq % WINDOW == 0 and d % LANES == 0
    tm, tt, chunk = _block_sizes(batch, seq)
    assert n % tm == 0 and seq % tt == 0 and n % chunk == 0 and chunk % tm == 0

    x2 = x.reshape(n, d)
    u, q, kv, gates, wgb, wub, wdb = _inproj(x2, g_mix, w_in, b_in, w_gate, w_up, w_down,
                                             c_conv=c_conv, n_q=n_q, n_kv=n_kv, tm=tm)
    act = _conv(u.reshape(batch, seq, c_conv), conv_w, conv_b, conv_ln_g, conv_ln_b, tt=tt)
    o = _attn(q.reshape(batch, seq, n_q), kv.reshape(batch, seq, 2 * n_kv), attn_sinks, n_heads,
              nsub=2 if seq % (2 * WINDOW) == 0 else 1)
    x1, hr, e, r, rw, counts = _merge(
        x2, act.reshape(n, c_conv), o.reshape(n, n_q), gates, w_conv_out, b_conv_out, w_attn_out, b_attn_out,
        w_out, g_ffn, w_router, b_router, tm=tm, chunk=chunk)
    pos, wflat = _pos(counts, e, r, rw, chunk=chunk)
    out = _moe(counts.reshape(n // chunk, n_exp), pos, wflat, hr,
               wgb, b_gate.reshape(n_exp, 1, -1), wub, b_up.reshape(n_exp, 1, -1),
               wdb, b_down.reshape(n_exp, 1, -1), x1, g_final, chunk=chunk, tsub=tm, d=d, bm=128)
    return out.reshape(batch, seq, d)
```

```python
import functools
import math

import jax
import jax.numpy as jnp
from jax import lax
from jax.experimental import pallas as pl
from jax.experimental.pallas import tpu as pltpu

EPS = 1e-5
K_CONV = 31
HEAD_DIM = 64
KV_GROUP = 8
WINDOW = 128
TOP_K = 4
SWIGLU_LIMIT = 7.0
SWIGLU_ALPHA = 1.702

LANES = 128
SUBLANES = 8
HALO = 32
BLOCK_STEP = 32
V7X_VMEM_BYTES = 64 * 1024 * 1024
VMEM_LIMIT = 56 * 1024 * 1024
VMEM_LIMIT_MOE = V7X_VMEM_BYTES - 2 * 1024 * 1024

F32 = jnp.float32
BF16 = jnp.bfloat16


def _const_spec(shape):
    nd = len(shape)
    return pl.BlockSpec(shape, lambda *_: (0,) * nd, pipeline_mode=pl.Buffered(1))


def _inproj_kernel(x_ref, g_ref, w_ref, b_ref, wg_ref, wu_ref, wd_ref,
                   u_ref, q_ref, kv_ref, gate_ref, wgb_ref, wub_ref, wdb_ref, *, c_conv, n_q, n_kv, d_model):
    wgb_ref[...] = wg_ref[...].astype(BF16)
    wub_ref[...] = wu_ref[...].astype(BF16)
    wdb_ref[...] = wd_ref[...].astype(BF16)

    x = x_ref[...]
    ms = jnp.mean(x * x, axis=-1, keepdims=True)
    h = (x * lax.rsqrt(ms + EPS) * g_ref[...]).astype(BF16)
    cw = 512

    def seg(lo, width):
        return jnp.dot(h, w_ref[:, lo:lo + width], preferred_element_type=F32) + b_ref[:, lo:lo + width]

    for c0 in range(0, c_conv, cw):
        a = seg(c0, cw)
        b = seg(c_conv + c0, cw)
        u_ref[:, c0:c0 + cw] = (a * jax.nn.sigmoid(b)).astype(BF16)
    off = 2 * c_conv
    scale = 1.0 / math.sqrt(HEAD_DIM)
    for c0 in range(0, n_q, cw):
        q_ref[:, c0:c0 + cw] = (seg(off + c0, cw) * scale).astype(BF16)
    off += n_q
    kv_ref[...] = seg(off, 2 * n_kv).astype(BF16)
    off += 2 * n_kv
    for c0 in range(0, 2 * d_model, cw):
        gate_ref[:, c0:c0 + cw] = jax.nn.sigmoid(seg(off + c0, cw)).astype(BF16)


def _inproj(x2, g_mix, w_in, b_in, w_gate, w_up, w_down, *, c_conv, n_q, n_kv, tm):
    n, d = x2.shape
    n_in = w_in.shape[1]
    steps = n // tm
    slabs = []
    for w in (w_gate, w_up, w_down):
        rows = w.shape[0] * w.shape[1]
        assert rows % steps == 0 and (rows // steps) % 16 == 0
        slabs.append((rows // steps, w.shape[2]))
    kern = functools.partial(_inproj_kernel, c_conv=c_conv, n_q=n_q, n_kv=n_kv, d_model=d)
    row = lambda i: (i, 0)
    outs = pl.pallas_call(
        kern,
        grid=(steps,),
        in_specs=[
            pl.BlockSpec((tm, d), row),
            _const_spec((1, d)),
            _const_spec((d, n_in)),
            _const_spec((1, n_in)),
        ] + [pl.BlockSpec(sl, row) for sl in slabs],
        out_specs=[
            pl.BlockSpec((tm, c_conv), row),
            pl.BlockSpec((tm, n_q), row),
            pl.BlockSpec((tm, 2 * n_kv), row),
            pl.BlockSpec((tm, 2 * d), row),
        ] + [pl.BlockSpec(sl, row) for sl in slabs],
        out_shape=[
            jax.ShapeDtypeStruct((n, c_conv), BF16),
            jax.ShapeDtypeStruct((n, n_q), BF16),
            jax.ShapeDtypeStruct((n, 2 * n_kv), BF16),
            jax.ShapeDtypeStruct((n, 2 * d), BF16),
        ] + [jax.ShapeDtypeStruct((w.shape[0] * w.shape[1], w.shape[2]), BF16) for w in (w_gate, w_up, w_down)],
        compiler_params=pltpu.CompilerParams(dimension_semantics=("arbitrary",), vmem_limit_bytes=VMEM_LIMIT),
        name="inproj",
    )(x2, g_mix.reshape(1, d), w_in.astype(BF16), b_in.reshape(1, n_in),
      *[w.reshape(w.shape[0] * w.shape[1], w.shape[2]) for w in (w_gate, w_up, w_down)])
    u, q, kv, gates = outs[:4]
    wgb, wub, wdb = [o.reshape(w.shape) for o, w in zip(outs[4:], (w_gate, w_up, w_down))]
    return u, q, kv, gates, wgb, wub, wdb


def _conv_kernel(halo_ref, u_ref, w_ref, cb_ref, lg_ref, lb_ref, act_ref, ext_ref, conv_ref, *, tt, strip):
    i = pl.program_id(1)
    rows = tt + HALO
    halo = jnp.where(i > 0, halo_ref[...].astype(F32), 0.0)
    ext = jnp.concatenate([halo, u_ref[...].astype(F32)], axis=0)
    ext_ref[0] = ext
    for s in range(1, SUBLANES):
        ext_ref[s] = pltpu.roll(ext, rows - s, axis=0)

    rgrp = 16
    rchunk = rgrp * SUBLANES
    for cb in range(ext.shape[1] // LANES):
        lanes = slice(cb * LANES, (cb + 1) * LANES)

        def taps(rc, carry, lanes=lanes):
            base = pl.multiple_of(rc * rchunk, rchunk)
            accs = [jnp.zeros((SUBLANES, LANES), F32) for _ in range(rgrp)]
            for j in range(K_CONV):
                a, s = divmod(HALO - (K_CONV - 1) + j, SUBLANES)
                wv = w_ref[j, :, lanes]
                for g in range(rgrp):
                    accs[g] = accs[g] + ext_ref[s, pl.ds(base + SUBLANES * (a + g), SUBLANES), lanes] * wv
            for g in range(rgrp):
                conv_ref[pl.ds(base + SUBLANES * g, SUBLANES), lanes] = accs[g]
            return carry

        lax.fori_loop(0, tt // rchunk, taps, 0)

    for r in range(tt // strip):
        rows_r = slice(r * strip, (r + 1) * strip)
        v = conv_ref[rows_r, :] + cb_ref[...]
        mu = jnp.mean(v, axis=-1, keepdims=True)
        dv = v - mu
        var = jnp.mean(dv * dv, axis=-1, keepdims=True)
        y = dv * lax.rsqrt(var + EPS) * lg_ref[...] + lb_ref[...]
        act_ref[rows_r, :] = (y * jax.nn.sigmoid(y)).astype(BF16)


def _conv(u3, conv_w, conv_b, ln_g, ln_b, *, tt):
    b, t, c = u3.shape
    hb = tt // HALO
    kern = functools.partial(_conv_kernel, tt=tt, strip=32)
    wpad = jnp.broadcast_to(conv_w[:, None, :], (K_CONV, SUBLANES, c))
    return pl.pallas_call(
        kern,
        grid=(b, t // tt),
        in_specs=[
            pl.BlockSpec((None, HALO, c), lambda bi, i: (bi, jnp.maximum(i * hb - 1, 0), 0)),
            pl.BlockSpec((None, tt, c), lambda bi, i: (bi, i, 0)),
            _const_spec((K_CONV, SUBLANES, c)),
            _const_spec((1, c)),
            _const_spec((1, c)),
            _const_spec((1, c)),
        ],
        out_specs=pl.BlockSpec((None, tt, c), lambda bi, i: (bi, i, 0)),
        out_shape=jax.ShapeDtypeStruct((b, t, c), BF16),
        scratch_shapes=[pltpu.VMEM((SUBLANES, tt + HALO, c), F32), pltpu.VMEM((tt, c), F32)],
        compiler_params=pltpu.CompilerParams(dimension_semantics=("arbitrary", "arbitrary"),
                                             vmem_limit_bytes=VMEM_LIMIT),
        name="conv",
    )(u3, u3, wpad, conv_b.reshape(1, c), ln_g.reshape(1, c), ln_b.reshape(1, c))


def _attn_kernel(q_ref, kvp_ref, kvc_ref, bias_ref, sink_ref, o_ref, *, nsub):
    i = pl.program_id(1)
    qb = WINDOW
    lane = lax.broadcasted_iota(jnp.int32, (2 * qb, LANES), 1)
    lo_half = lane < HEAD_DIM
    col = lax.broadcasted_iota(jnp.int32, (qb, 2 * qb), 1)
    ind_r = lax.broadcasted_iota(jnp.int32, (4 * qb, LANES), 0) < 2 * qb
    ind_c = lax.broadcasted_iota(jnp.int32, (4 * qb, LANES), 1) < HEAD_DIM
    ind = jnp.where(ind_r == ind_c, 1.0, 0.0).astype(BF16)
    lo_lane = lax.broadcasted_iota(jnp.int32, (qb, LANES), 1) < HEAD_DIM
    kv_blocks = [kvp_ref[...].astype(F32)] + [kvc_ref[b * qb:(b + 1) * qb, :].astype(F32) for b in range(nsub)]

    for b in range(nsub):
        rows = slice(b * qb, (b + 1) * qb)
        kvp, kvc = kv_blocks[b], kv_blocks[b + 1]
        kband = jnp.concatenate([kvp[:, :LANES], kvc[:, :LANES]], axis=0)
        vband = jnp.concatenate([kvp[:, LANES:], kvc[:, LANES:]], axis=0)
        kswap = pltpu.roll(kband, HEAD_DIM, axis=1)
        vswap = pltpu.roll(vband, HEAD_DIM, axis=1)
        if b == 0:
            first_block_mask = jnp.where((i == 0) & (col < qb), -jnp.inf, 0.0).astype(F32)
        for h in range(2):
            ksrc_lo, ksrc_hi = (kband, kswap) if h == 0 else (kswap, kband)
            vsrc_lo, vsrc_hi = (vband, vswap) if h == 0 else (vswap, vband)
            kext = jnp.concatenate([jnp.where(lo_half, ksrc_lo, 0.0), jnp.where(lo_half, 0.0, ksrc_hi)],
                                   axis=0).astype(BF16)
            vext = jnp.concatenate([jnp.where(lo_half, vsrc_lo, 0.0), jnp.where(lo_half, 0.0, vsrc_hi)],
                                   axis=0).astype(BF16)
            vext2 = jnp.concatenate([vext, ind], axis=1)
            for j in range(4):
                cols = slice((4 * h + j) * LANES, (4 * h + j + 1) * LANES)
                s = lax.dot_general(q_ref[rows, cols], kext, (((1,), (1,)), ((), ())),
                                    preferred_element_type=F32)
                ps, es = [], []
                for half in range(2):
                    sh = s[:, half * 2 * qb:(half + 1) * 2 * qb] + bias_ref[h, half, j]
                    if b == 0:
                        sh = sh + first_block_mask
                    sink = sink_ref[8 * h + 2 * j + half]
                    m = jnp.maximum(jnp.max(sh, axis=-1, keepdims=True), sink)
                    ps.append(jnp.exp(sh - m).astype(BF16))
                    es.append(jnp.exp(sink - m))
                r = jnp.dot(jnp.concatenate(ps, axis=1), vext2, preferred_element_type=F32)
                den = r[:, LANES:] + jnp.where(lo_lane, es[0], es[1])
                o_ref[rows, cols] = (r[:, :LANES] / den).astype(BF16)


def _alibi_bias(n_heads):
    qb = WINDOW
    slopes = jnp.asarray([2.0 ** (-8.0 * (i + 1) / n_heads) for i in range(n_heads)], dtype=F32)
    qi = jnp.arange(qb)[:, None]
    sj = jnp.arange(2 * qb)[None, :]
    dist = (qi + qb - sj).astype(F32)
    valid = (dist >= 0) & (dist < WINDOW)
    heads = (8 * jnp.arange(2)[:, None, None] + jnp.arange(2)[None, :, None] + 2 * jnp.arange(4)[None, None, :])
    sl = slopes[heads]
    return jnp.where(valid[None, None, None], -sl[..., None, None] * dist, -jnp.inf)


def _attn(q3, kv3, attn_sinks, n_heads, *, nsub):
    b, t, nq = q3.shape
    qb = WINDOW
    ta = nsub * qb
    bias = _alibi_bias(n_heads)
    return pl.pallas_call(
        functools.partial(_attn_kernel, nsub=nsub),
        grid=(b, t // ta),
        in_specs=[
            pl.BlockSpec((None, ta, nq), lambda bi, i: (bi, i, 0)),
            pl.BlockSpec((None, qb, kv3.shape[2]), lambda bi, i: (bi, jnp.maximum(i * nsub - 1, 0), 0)),
            pl.BlockSpec((None, ta, kv3.shape[2]), lambda bi, i: (bi, i, 0)),
            _const_spec(bias.shape),
            pl.BlockSpec(memory_space=pltpu.SMEM),
        ],
        out_specs=pl.BlockSpec((None, ta, nq), lambda bi, i: (bi, i, 0)),
        out_shape=jax.ShapeDtypeStruct((b, t, nq), BF16),
        compiler_params=pltpu.CompilerParams(dimension_semantics=("arbitrary", "arbitrary"),
                                             vmem_limit_bytes=VMEM_LIMIT),
        name="attn",
    )(q3, kv3, kv3, bias, attn_sinks.astype(F32))


def _merge_kernel(x_ref, act_ref, o_ref, gate_ref, wco_ref, bco_ref, wao_ref, bao_ref, wout_ref,
                  gffn_ref, wrt_ref, brt_ref,
                  x1_ref, hr_ref, e_ref, r_ref, rw_ref, cnt_ref,
                  carry_ref, *, tm, d, n_exp, tiles_per_chunk):
    i = pl.program_id(0)
    yc = jnp.dot(act_ref[...], wco_ref[...], preferred_element_type=F32) + bco_ref[...]
    ya = jnp.dot(o_ref[...], wao_ref[...], preferred_element_type=F32) + bao_ref[...]
    m = gate_ref[:, :d].astype(F32) * yc + gate_ref[:, d:].astype(F32) * ya
    x1 = x_ref[...] + jnp.dot(m.astype(BF16), wout_ref[...], preferred_element_type=F32)
    x1_ref[...] = x1
    ms = jnp.mean(x1 * x1, axis=-1, keepdims=True)
    h2 = x1 * lax.rsqrt(ms + EPS) * gffn_ref[...]
    nblk = d // LANES
    for cb in range(nblk):
        hr_ref[pl.ds(cb, tm, stride=nblk), :] = h2[:, cb * LANES:(cb + 1) * LANES]

    h_hi = h2.astype(BF16)
    h_lo = (h2 - h_hi.astype(F32)).astype(BF16)
    wt = wrt_ref[...]
    wt_hi = wt.astype(BF16)
    wt_lo = (wt - wt_hi.astype(F32)).astype(BF16)
    nt = (((1,), (1,)), ((), ()))
    l = (lax.dot_general(wt_hi, h_hi, nt, preferred_element_type=F32)
         + lax.dot_general(wt_lo, h_hi, nt, preferred_element_type=F32)
         + lax.dot_general(wt_hi, h_lo, nt, preferred_element_type=F32)) + brt_ref[...]

    iota = lax.broadcasted_iota(jnp.int32, (n_exp, tm), 0)
    vals, idxs, sels = [], [], []
    for _ in range(TOP_K):
        mk = jnp.max(l, axis=0, keepdims=True)
        ik = jnp.min(jnp.where(l == mk, iota, n_exp), axis=0, keepdims=True)
        sel = iota == ik
        l = jnp.where(sel, -jnp.inf, l)
        vals.append(mk)
        idxs.append(ik)
        sels.append(sel)
    exs = [jnp.exp(v - vals[0]) for v in vals]
    den = exs[0] + exs[1] + exs[2] + exs[3]

    @pl.when(i % tiles_per_chunk == 0)
    def _():
        carry_ref[...] = jnp.zeros_like(carry_ref)

    member = (sels[0] | sels[1] | sels[2] | sels[3])
    mf = jnp.where(member, 1.0, 0.0).astype(F32)
    r_i = lax.broadcasted_iota(jnp.int32, (tm, tm), 0)
    c_i = lax.broadcasted_iota(jnp.int32, (tm, tm), 1)
    earlier = jnp.where(r_i < c_i, 1.0, 0.0).astype(BF16)
    before = jnp.dot(mf.astype(BF16), earlier, preferred_element_type=F32) + carry_ref[...]
    carry_new = carry_ref[...] + jnp.sum(mf, axis=1, keepdims=True)
    carry_ref[...] = carry_new
    cnt_ref[...] = carry_new.astype(jnp.int32)

    ranks = [jnp.sum(jnp.where(sels[k], before, 0.0), axis=0, keepdims=True).astype(jnp.int32)
             for k in range(TOP_K)]
    e_ref[...] = jnp.concatenate(idxs, axis=0)
    r_ref[...] = jnp.concatenate(ranks, axis=0)
    rw_ref[...] = jnp.concatenate([exs[k] / den for k in range(TOP_K)], axis=0)


def _merge(x2, act, o, gates, w_conv_out, b_conv_out, w_attn_out, b_attn_out, w_out, g_ffn, w_router, b_router,
           *, tm, chunk):
    n, d = x2.shape
    n_exp = w_router.shape[1]
    tiles_per_chunk = chunk // tm
    n_chunks = n // chunk
    nblk = d // LANES
    kern = functools.partial(_merge_kernel, tm=tm, d=d, n_exp=n_exp, tiles_per_chunk=tiles_per_chunk)
    row = lambda i: (i, 0)
    return pl.pallas_call(
        kern,
        grid=(n // tm,),
        in_specs=[
            pl.BlockSpec((tm, d), row),
            pl.BlockSpec((tm, d), row),
            pl.BlockSpec((tm, d), row),
            pl.BlockSpec((tm, 2 * d), row),
            _const_spec((d, d)), _const_spec((1, d)),
            _const_spec((d, d)), _const_spec((1, d)),
            _const_spec((d, d)),
            _const_spec((1, d)),
            _const_spec((n_exp, d)), _const_spec((n_exp, 1)),
        ],
        out_specs=[
            pl.BlockSpec((tm, d), row),
            pl.BlockSpec((tm * nblk, LANES), row),
            pl.BlockSpec((TOP_K, tm), lambda i: (0, i)),
            pl.BlockSpec((TOP_K, tm), lambda i: (0, i)),
            pl.BlockSpec((TOP_K, tm), lambda i: (0, i)),
            pl.BlockSpec((None, n_exp, 1), lambda i: (i // tiles_per_chunk, 0, 0)),
        ],
        out_shape=[
            jax.ShapeDtypeStruct((n, d), F32),
            jax.ShapeDtypeStruct((n * nblk, LANES), F32),
            jax.ShapeDtypeStruct((TOP_K, n), jnp.int32),
            jax.ShapeDtypeStruct((TOP_K, n), jnp.int32),
            jax.ShapeDtypeStruct((TOP_K, n), F32),
            jax.ShapeDtypeStruct((n_chunks, n_exp, 1), jnp.int32),
        ],
        scratch_shapes=[pltpu.VMEM((n_exp, 1), F32)],
        compiler_params=pltpu.CompilerParams(dimension_semantics=("arbitrary",), vmem_limit_bytes=VMEM_LIMIT),
        name="merge",
    )(x2, act, o, gates, w_conv_out.astype(BF16), b_conv_out.reshape(1, d), w_attn_out.astype(BF16),
      b_attn_out.reshape(1, d), w_out.astype(BF16), g_ffn.reshape(1, d), w_router.T, b_router.reshape(n_exp, 1))


def _moe_kernel(cnt_ref, pos_ref, w_ref, h_ref, wg_ref, bg_ref, wu_ref, bu_ref, wd_ref, bd_ref, x1_ref, gf_ref,
                out_ref, xs_ref, off_ref, ys_ref, *, chunk, tsub, d, n_exp, bm):
    c = pl.program_id(0)
    s = pl.program_id(1)
    nblk = d // LANES
    n_sub = chunk // tsub

    @pl.when((c == 0) & (s == 0))
    def _():
        xs_ref[...] = jnp.zeros_like(xs_ref)

    @pl.when(s == 0)
    def _():
        def offs(j, acc):
            off_ref[j] = acc
            return acc + cnt_ref[c, j]

        lax.fori_loop(0, n_exp, offs, 0)

    @pl.when(s < n_sub)
    def _():
        t0 = s * tsub

        def dispatch(i, carry):
            row = h_ref[pl.ds(pl.multiple_of(i * nblk, nblk), nblk), :]
            for k in range(TOP_K):
                row0 = pos_ref[k * chunk + t0 + i]
                xs_ref[pl.ds(pl.multiple_of(row0, nblk), nblk), :] = row
            return carry

        lax.fori_loop(0, tsub, dispatch, 0, unroll=4)

    @pl.when((s >= n_sub) & (s < n_sub + n_exp))
    def _():
        e = s - n_sub
        cnt = cnt_ref[c, e]
        base = off_ref[e]

        def ffn(slot0, m, n_valid=None):
            row0 = slot0 * nblk
            x = jnp.concatenate([xs_ref[pl.ds(row0 + cb, m, stride=nblk), :].astype(BF16) for cb in range(nblk)],
                                axis=1)
            g = jnp.dot(x, wg_ref[...], preferred_element_type=F32) + bg_ref[...]
            u = jnp.dot(x, wu_ref[...], preferred_element_type=F32) + bu_ref[...]
            g = jnp.minimum(g, SWIGLU_LIMIT)
            u = jnp.clip(u, -SWIGLU_LIMIT, SWIGLU_LIMIT)
            a = (u + 1.0) * (g * jax.nn.sigmoid(SWIGLU_ALPHA * g))
            y = jnp.dot(a.astype(BF16), wd_ref[...], preferred_element_type=F32) + bd_ref[...]
            if n_valid is not None:
                keep = lax.broadcasted_iota(jnp.int32, (m, LANES), 0) < n_valid
            for cb in range(nblk):
                dst = pl.ds(row0 + cb, m, stride=nblk)
                ycb = y[:, cb * LANES:(cb + 1) * LANES]
                xs_ref[dst, :] = ycb if n_valid is None else jnp.where(keep, ycb, xs_ref[dst, :])

        n_big = jnp.maximum((cnt - (bm + 1)) // (2 * bm), 0)

        def big(b, carry):
            ffn(base + b * 2 * bm, 2 * bm)
            return carry

        lax.fori_loop(0, n_big, big, 0)
        rem = cnt - n_big * 2 * bm
        start = base + n_big * 2 * bm
        mid = jnp.where(rem > 2 * bm, bm, 0)

        @pl.when(rem > 2 * bm)
        def _():
            ffn(start, bm)

        rem = rem - mid
        start = start + mid
        for m in range(bm, 2 * bm + 1, BLOCK_STEP):
            lo = 0 if m == bm else m - BLOCK_STEP

            @pl.when((rem > lo) & (rem <= m))
            def _(m=m):
                ffn(start, m, rem)

    @pl.when(s >= n_sub + n_exp)
    def _():
        t0 = (s - n_sub - n_exp) * tsub

        def combine(i, carry):
            acc = jnp.zeros((nblk, LANES), F32)
            for k in range(TOP_K):
                a = k * chunk + t0 + i
                slot = pl.ds(pl.multiple_of(pos_ref[a], nblk), nblk)
                acc = acc + w_ref[a] * xs_ref[slot, :]
            ys_ref[pl.ds(pl.multiple_of(i * nblk, nblk), nblk), :] = acc
            return carry

        lax.fori_loop(0, tsub, combine, 0, unroll=4)
        y = jnp.concatenate([ys_ref[pl.ds(cb, tsub, stride=nblk), :] for cb in range(nblk)], axis=1)
        x = x1_ref[...] + y
        ms = jnp.mean(x * x, axis=-1, keepdims=True)
        out_ref[...] = x * lax.rsqrt(ms + EPS) * gf_ref[...]


def _moe(counts, pos, rw, hr, wg, bg, wu, bu, wd, bd, x1, g_final, *, chunk, tsub, d, bm):
    n_chunks, n_exp = counts.shape
    nblk = d // LANES
    f = wg.shape[2]
    n_assign = chunk * TOP_K
    n_sub = chunk // tsub
    kern = functools.partial(_moe_kernel, chunk=chunk, tsub=tsub, d=d, n_exp=n_exp, bm=bm)
    smem_blk = lambda: pl.BlockSpec((n_assign,), lambda c, s, cnt: (c,), memory_space=pltpu.SMEM)
    wspec = lambda shp: pl.BlockSpec((None,) + shp, lambda c, s, cnt: (jnp.clip(s - n_sub, 0, n_exp - 1), 0, 0))
    out_tile = lambda c, s, cnt: (c * n_sub + jnp.clip(s - n_sub - n_exp, 0, n_sub - 1), 0)
    grid_spec = pltpu.PrefetchScalarGridSpec(
        num_scalar_prefetch=1,
        grid=(n_chunks, 2 * n_sub + n_exp),
        in_specs=[
            smem_blk(), smem_blk(),
            pl.BlockSpec((tsub * nblk, LANES), lambda c, s, cnt: (c * n_sub + jnp.minimum(s, n_sub - 1), 0)),
            wspec((d, f)), wspec((1, f)), wspec((d, f)), wspec((1, f)), wspec((f, d)), wspec((1, d)),
            pl.BlockSpec((tsub, d), out_tile),
            pl.BlockSpec((1, d), lambda c, s, cnt: (0, 0)),
        ],
        out_specs=pl.BlockSpec((tsub, d), out_tile),
        scratch_shapes=[
            pltpu.VMEM(((n_assign + 2 * bm) * nblk, LANES), F32),
            pltpu.SMEM((n_exp,), jnp.int32),
            pltpu.VMEM((tsub * nblk, LANES), F32),
        ],
    )
    return pl.pallas_call(
        kern,
        grid_spec=grid_spec,
        out_shape=jax.ShapeDtypeStruct((n_chunks * chunk, d), F32),
        compiler_params=pltpu.CompilerParams(dimension_semantics=("arbitrary", "arbitrary"),
                                             vmem_limit_bytes=VMEM_LIMIT_MOE),
        name="moe",
    )(counts, pos, rw, hr, wg, bg, wu, bu, wd, bd, x1, g_final.reshape(1, d))


def _pos_kernel(cnt_ref, e_ref, r_ref, w_ref, pos_ref, wflat_ref, *, n_exp, chunk, nblk):
    cnt = jnp.broadcast_to(cnt_ref[...], (n_exp, LANES))
    hi = (cnt // 256).astype(F32).astype(BF16)
    lo = (cnt % 256).astype(F32).astype(BF16)
    r_i = lax.broadcasted_iota(jnp.int32, (n_exp, n_exp), 0)
    c_i = lax.broadcasted_iota(jnp.int32, (n_exp, n_exp), 1)
    tri = jnp.where(c_i < r_i, 1.0, 0.0).astype(BF16)
    off = (256.0 * jnp.dot(tri, hi, preferred_element_type=F32)
           + jnp.dot(tri, lo, preferred_element_type=F32)).astype(jnp.int32)[:, 0:1]
    e = e_ref[...]
    pos = r_ref[...]
    for j in range(n_exp):
        pos = pos + jnp.where(e == j, off[j:j + 1, :], 0)
    for k in range(TOP_K):
        pos_ref[:, k * chunk:(k + 1) * chunk] = pos[k:k + 1, :] * nblk
        wflat_ref[:, k * chunk:(k + 1) * chunk] = w_ref[k:k + 1, :]


def _pos(counts, e, r, rw, *, chunk, nblk):
    n_chunks, n_exp, _ = counts.shape
    assert chunk < 256 * 256 and chunk % LANES == 0
    blk = pl.BlockSpec((TOP_K, chunk), lambda c: (0, c))
    flat = pl.BlockSpec((None, 1, TOP_K * chunk), lambda c: (c, 0, 0))
    pos, wflat = pl.pallas_call(
        functools.partial(_pos_kernel, n_exp=n_exp, chunk=chunk, nblk=nblk),
        grid=(n_chunks,),
        in_specs=[pl.BlockSpec((None, n_exp, 1), lambda c: (c, 0, 0)), blk, blk, blk],
        out_specs=[flat, flat],
        out_shape=[jax.ShapeDtypeStruct((n_chunks, 1, TOP_K * chunk), jnp.int32),
                   jax.ShapeDtypeStruct((n_chunks, 1, TOP_K * chunk), F32)],
        compiler_params=pltpu.CompilerParams(dimension_semantics=("arbitrary",), vmem_limit_bytes=VMEM_LIMIT),
        name="pos",
    )(counts, e, r, rw)
    return pos.reshape(-1), wflat.reshape(-1)


def _block_sizes(batch, seq):
    n = batch * seq
    tm = min(512, n)
    tt = min(512, seq)
    chunk = min(2048, n)
    return tm, tt, chunk


def kernel(x, g_mix, w_in, b_in, conv_w, conv_b, conv_ln_g, conv_ln_b, w_conv_out, b_conv_out, attn_sinks,
           w_attn_out, b_attn_out, w_out, g_ffn, w_router, b_router, w_gate, b_gate, w_up, b_up, w_down,
           b_down, g_final):
    batch, seq, d = x.shape
    n = batch * seq
    c_conv = conv_w.shape[1]
    n_heads = attn_sinks.shape[0]
    n_q = n_heads * HEAD_DIM
    n_kv = (n_heads // KV_GROUP) * HEAD_DIM
    n_exp = w_router.shape[1]
    assert n_kv == LANES and seq % WINDOW == 0 and d % LANES == 0
    tm, tt, chunk = _block_sizes(batch, seq)
    assert n % tm == 0 and seq % tt == 0 and n % chunk == 0 and chunk % tm == 0

    x2 = x.reshape(n, d)
    u, q, kv, gates, wgb, wub, wdb = _inproj(x2, g_mix, w_in, b_in, w_gate, w_up, w_down,
                                             c_conv=c_conv, n_q=n_q, n_kv=n_kv, tm=tm)
    act = _conv(u.reshape(batch, seq, c_conv), conv_w, conv_b, conv_ln_g, conv_ln_b, tt=tt)
    o = _attn(q.reshape(batch, seq, n_q), kv.reshape(batch, seq, 2 * n_kv), attn_sinks, n_heads,
              nsub=max(ns for ns in (1, 2, 4) if seq % (ns * WINDOW) == 0))
    x1, hr, e, r, rw, counts = _merge(
        x2, act.reshape(n, c_conv), o.reshape(n, n_q), gates, w_conv_out, b_conv_out, w_attn_out, b_attn_out,
        w_out, g_ffn, w_router, b_router, tm=tm, chunk=chunk)
    pos, wflat = _pos(counts, e, r, rw, chunk=chunk, nblk=d // LANES)
    out = _moe(counts.reshape(n // chunk, n_exp), pos, wflat, hr,
               wgb, b_gate.reshape(n_exp, 1, -1), wub, b_up.reshape(n_exp, 1, -1),
               wdb, b_down.reshape(n_exp, 1, -1), x1, g_final, chunk=chunk, tsub=tm, d=d, bm=128)
    return out.reshape(batch, seq, d)
```

```python
import functools
import math

import jax
import jax.numpy as jnp
from jax import lax
from jax.experimental import pallas as pl
from jax.experimental.pallas import tpu as pltpu

EPS = 1e-5
K_CONV = 31
HEAD_DIM = 64
KV_GROUP = 8
WINDOW = 128
TOP_K = 4
SWIGLU_LIMIT = 7.0
SWIGLU_ALPHA = 1.702

LANES = 128
SUBLANES = 8
HALO = 32
BLOCK_STEP = 32
V7X_VMEM_BYTES = 64 * 1024 * 1024
VMEM_LIMIT = 56 * 1024 * 1024
VMEM_LIMIT_MOE = V7X_VMEM_BYTES - 2 * 1024 * 1024

F32 = jnp.float32
BF16 = jnp.bfloat16


def _const_spec(shape):
    nd = len(shape)
    return pl.BlockSpec(shape, lambda *_: (0,) * nd, pipeline_mode=pl.Buffered(1))


def _inproj_kernel(x_ref, g_ref, w_ref, b_ref, wg_ref, wu_ref, wd_ref,
                   u_ref, q_ref, kv_ref, gate_ref, wgb_ref, wub_ref, wdb_ref, *, c_conv, n_q, n_kv, d_model):
    wgb_ref[...] = wg_ref[...].astype(BF16)
    wub_ref[...] = wu_ref[...].astype(BF16)
    wdb_ref[...] = wd_ref[...].astype(BF16)

    x = x_ref[...]
    ms = jnp.mean(x * x, axis=-1, keepdims=True)
    h = (x * lax.rsqrt(ms + EPS) * g_ref[...]).astype(BF16)
    cw = 512

    def seg(lo, width):
        return jnp.dot(h, w_ref[:, lo:lo + width], preferred_element_type=F32) + b_ref[:, lo:lo + width]

    for c0 in range(0, c_conv, cw):
        a = seg(c0, cw)
        b = seg(c_conv + c0, cw)
        u_ref[:, c0:c0 + cw] = (a * jax.nn.sigmoid(b)).astype(BF16)
    off = 2 * c_conv
    scale = 1.0 / math.sqrt(HEAD_DIM)
    for c0 in range(0, n_q, cw):
        q_ref[:, c0:c0 + cw] = (seg(off + c0, cw) * scale).astype(BF16)
    off += n_q
    kv_ref[...] = seg(off, 2 * n_kv).astype(BF16)
    off += 2 * n_kv
    for c0 in range(0, 2 * d_model, cw):
        gate_ref[:, c0:c0 + cw] = jax.nn.sigmoid(seg(off + c0, cw)).astype(BF16)


def _inproj(x2, g_mix, w_in, b_in, w_gate, w_up, w_down, *, c_conv, n_q, n_kv, tm):
    n, d = x2.shape
    n_in = w_in.shape[1]
    steps = n // tm
    slabs = []
    for w in (w_gate, w_up, w_down):
        rows = w.shape[0] * w.shape[1]
        assert rows % steps == 0 and (rows // steps) % 16 == 0
        slabs.append((rows // steps, w.shape[2]))
    kern = functools.partial(_inproj_kernel, c_conv=c_conv, n_q=n_q, n_kv=n_kv, d_model=d)
    row = lambda i: (i, 0)
    outs = pl.pallas_call(
        kern,
        grid=(steps,),
        in_specs=[
            pl.BlockSpec((tm, d), row),
            _const_spec((1, d)),
            _const_spec((d, n_in)),
            _const_spec((1, n_in)),
        ] + [pl.BlockSpec(sl, row) for sl in slabs],
        out_specs=[
            pl.BlockSpec((tm, c_conv), row),
            pl.BlockSpec((tm, n_q), row),
            pl.BlockSpec((tm, 2 * n_kv), row),
            pl.BlockSpec((tm, 2 * d), row),
        ] + [pl.BlockSpec(sl, row) for sl in slabs],
        out_shape=[
            jax.ShapeDtypeStruct((n, c_conv), BF16),
            jax.ShapeDtypeStruct((n, n_q), BF16),
            jax.ShapeDtypeStruct((n, 2 * n_kv), BF16),
            jax.ShapeDtypeStruct((n, 2 * d), BF16),
        ] + [jax.ShapeDtypeStruct((w.shape[0] * w.shape[1], w.shape[2]), BF16) for w in (w_gate, w_up, w_down)],
        compiler_params=pltpu.CompilerParams(dimension_semantics=("arbitrary",), vmem_limit_bytes=VMEM_LIMIT),
        name="inproj",
    )(x2, g_mix.reshape(1, d), w_in.astype(BF16), b_in.reshape(1, n_in),
      *[w.reshape(w.shape[0] * w.shape[1], w.shape[2]) for w in (w_gate, w_up, w_down)])
    u, q, kv, gates = outs[:4]
    wgb, wub, wdb = [o.reshape(w.shape) for o, w in zip(outs[4:], (w_gate, w_up, w_down))]
    return u, q, kv, gates, wgb, wub, wdb


def _conv_kernel(halo_ref, u_ref, w_ref, cb_ref, lg_ref, lb_ref, act_ref, ext_ref, conv_ref, *, tt, strip):
    i = pl.program_id(1)
    rows = tt + HALO
    halo = jnp.where(i > 0, halo_ref[...].astype(F32), 0.0)
    ext = jnp.concatenate([halo, u_ref[...].astype(F32)], axis=0)
    ext_ref[0] = ext
    for s in range(1, SUBLANES):
        ext_ref[s] = pltpu.roll(ext, rows - s, axis=0)

    rgrp = 16
    rchunk = rgrp * SUBLANES
    for cb in range(ext.shape[1] // LANES):
        lanes = slice(cb * LANES, (cb + 1) * LANES)

        def taps(rc, carry, lanes=lanes):
            base = pl.multiple_of(rc * rchunk, rchunk)
            accs = [jnp.zeros((SUBLANES, LANES), F32) for _ in range(rgrp)]
            for j in range(K_CONV):
                a, s = divmod(HALO - (K_CONV - 1) + j, SUBLANES)
                wv = w_ref[j, :, lanes]
                for g in range(rgrp):
                    accs[g] = accs[g] + ext_ref[s, pl.ds(base + SUBLANES * (a + g), SUBLANES), lanes] * wv
            for g in range(rgrp):
                conv_ref[pl.ds(base + SUBLANES * g, SUBLANES), lanes] = accs[g]
            return carry

        lax.fori_loop(0, tt // rchunk, taps, 0)

    for r in range(tt // strip):
        rows_r = slice(r * strip, (r + 1) * strip)
        v = conv_ref[rows_r, :] + cb_ref[...]
        mu = jnp.mean(v, axis=-1, keepdims=True)
        dv = v - mu
        var = jnp.mean(dv * dv, axis=-1, keepdims=True)
        y = dv * lax.rsqrt(var + EPS) * lg_ref[...] + lb_ref[...]
        act_ref[rows_r, :] = (y * jax.nn.sigmoid(y)).astype(BF16)


def _conv(u3, conv_w, conv_b, ln_g, ln_b, *, tt):
    b, t, c = u3.shape
    hb = tt // HALO
    kern = functools.partial(_conv_kernel, tt=tt, strip=32)
    wpad = jnp.broadcast_to(conv_w[:, None, :], (K_CONV, SUBLANES, c))
    return pl.pallas_call(
        kern,
        grid=(b, t // tt),
        in_specs=[
            pl.BlockSpec((None, HALO, c), lambda bi, i: (bi, jnp.maximum(i * hb - 1, 0), 0)),
            pl.BlockSpec((None, tt, c), lambda bi, i: (bi, i, 0)),
            _const_spec((K_CONV, SUBLANES, c)),
            _const_spec((1, c)),
            _const_spec((1, c)),
            _const_spec((1, c)),
        ],
        out_specs=pl.BlockSpec((None, tt, c), lambda bi, i: (bi, i, 0)),
        out_shape=jax.ShapeDtypeStruct((b, t, c), BF16),
        scratch_shapes=[pltpu.VMEM((SUBLANES, tt + HALO, c), F32), pltpu.VMEM((tt, c), F32)],
        compiler_params=pltpu.CompilerParams(dimension_semantics=("arbitrary", "arbitrary"),
                                             vmem_limit_bytes=VMEM_LIMIT),
        name="conv",
    )(u3, u3, wpad, conv_b.reshape(1, c), ln_g.reshape(1, c), ln_b.reshape(1, c))


def _attn_kernel(q_ref, kvp_ref, kvc_ref, bias_ref, sink_ref, o_ref, *, nsub):
    i = pl.program_id(1)
    qb = WINDOW
    lane = lax.broadcasted_iota(jnp.int32, (2 * qb, LANES), 1)
    lo_half = lane < HEAD_DIM
    col = lax.broadcasted_iota(jnp.int32, (qb, 2 * qb), 1)
    ind_r = lax.broadcasted_iota(jnp.int32, (4 * qb, LANES), 0) < 2 * qb
    ind_c = lax.broadcasted_iota(jnp.int32, (4 * qb, LANES), 1) < HEAD_DIM
    ind = jnp.where(ind_r == ind_c, 1.0, 0.0).astype(BF16)
    lo_lane = lax.broadcasted_iota(jnp.int32, (qb, LANES), 1) < HEAD_DIM
    kv_blocks = [kvp_ref[...].astype(F32)] + [kvc_ref[b * qb:(b + 1) * qb, :].astype(F32) for b in range(nsub)]

    for b in range(nsub):
        rows = slice(b * qb, (b + 1) * qb)
        kvp, kvc = kv_blocks[b], kv_blocks[b + 1]
        kband = jnp.concatenate([kvp[:, :LANES], kvc[:, :LANES]], axis=0)
        vband = jnp.concatenate([kvp[:, LANES:], kvc[:, LANES:]], axis=0)
        kswap = pltpu.roll(kband, HEAD_DIM, axis=1)
        vswap = pltpu.roll(vband, HEAD_DIM, axis=1)
        if b == 0:
            first_block_mask = jnp.where((i == 0) & (col < qb), -jnp.inf, 0.0).astype(F32)
        for h in range(2):
            ksrc_lo, ksrc_hi = (kband, kswap) if h == 0 else (kswap, kband)
            vsrc_lo, vsrc_hi = (vband, vswap) if h == 0 else (vswap, vband)
            kext = jnp.concatenate([jnp.where(lo_half, ksrc_lo, 0.0), jnp.where(lo_half, 0.0, ksrc_hi)],
                                   axis=0).astype(BF16)
            vext = jnp.concatenate([jnp.where(lo_half, vsrc_lo, 0.0), jnp.where(lo_half, 0.0, vsrc_hi)],
                                   axis=0).astype(BF16)
            vext2 = jnp.concatenate([vext, ind], axis=1)
            for j in range(4):
                cols = slice((4 * h + j) * LANES, (4 * h + j + 1) * LANES)
                s = lax.dot_general(q_ref[rows, cols], kext, (((1,), (1,)), ((), ())),
                                    preferred_element_type=F32)
                ps, es = [], []
                for half in range(2):
                    sh = s[:, half * 2 * qb:(half + 1) * 2 * qb] + bias_ref[h, half, j]
                    if b == 0:
                        sh = sh + first_block_mask
                    sink = sink_ref[8 * h + 2 * j + half]
                    m = jnp.maximum(jnp.max(sh, axis=-1, keepdims=True), sink)
                    ps.append(jnp.exp(sh - m).astype(BF16))
                    es.append(jnp.exp(sink - m))
                r = jnp.dot(jnp.concatenate(ps, axis=1), vext2, preferred_element_type=F32)
                den = r[:, LANES:] + jnp.where(lo_lane, es[0], es[1])
                o_ref[rows, cols] = (r[:, :LANES] / den).astype(BF16)


def _alibi_bias(n_heads):
    qb = WINDOW
    slopes = jnp.asarray([2.0 ** (-8.0 * (i + 1) / n_heads) for i in range(n_heads)], dtype=F32)
    qi = jnp.arange(qb)[:, None]
    sj = jnp.arange(2 * qb)[None, :]
    dist = (qi + qb - sj).astype(F32)
    valid = (dist >= 0) & (dist < WINDOW)
    heads = (8 * jnp.arange(2)[:, None, None] + jnp.arange(2)[None, :, None] + 2 * jnp.arange(4)[None, None, :])
    sl = slopes[heads]
    return jnp.where(valid[None, None, None], -sl[..., None, None] * dist, -jnp.inf)


def _attn(q3, kv3, attn_sinks, n_heads, *, nsub):
    b, t, nq = q3.shape
    qb = WINDOW
    ta = nsub * qb
    bias = _alibi_bias(n_heads)
    return pl.pallas_call(
        functools.partial(_attn_kernel, nsub=nsub),
        grid=(b, t // ta),
        in_specs=[
            pl.BlockSpec((None, ta, nq), lambda bi, i: (bi, i, 0)),
            pl.BlockSpec((None, qb, kv3.shape[2]), lambda bi, i: (bi, jnp.maximum(i * nsub - 1, 0), 0)),
            pl.BlockSpec((None, ta, kv3.shape[2]), lambda bi, i: (bi, i, 0)),
            _const_spec(bias.shape),
            pl.BlockSpec(memory_space=pltpu.SMEM),
        ],
        out_specs=pl.BlockSpec((None, ta, nq), lambda bi, i: (bi, i, 0)),
        out_shape=jax.ShapeDtypeStruct((b, t, nq), BF16),
        compiler_params=pltpu.CompilerParams(dimension_semantics=("arbitrary", "arbitrary"),
                                             vmem_limit_bytes=VMEM_LIMIT),
        name="attn",
    )(q3, kv3, kv3, bias, attn_sinks.astype(F32))


def _merge_kernel(x_ref, act_ref, o_ref, gate_ref, wco_ref, bco_ref, wao_ref, bao_ref, wout_ref,
                  gffn_ref, wrt_ref, brt_ref,
                  x1_ref, hr_ref, e_ref, r_ref, rw_ref, cnt_ref,
                  carry_ref, *, tm, d, n_exp, tiles_per_chunk):
    i = pl.program_id(0)
    yc = jnp.dot(act_ref[...], wco_ref[...], preferred_element_type=F32) + bco_ref[...]
    ya = jnp.dot(o_ref[...], wao_ref[...], preferred_element_type=F32) + bao_ref[...]
    m = gate_ref[:, :d].astype(F32) * yc + gate_ref[:, d:].astype(F32) * ya
    x1 = x_ref[...] + jnp.dot(m.astype(BF16), wout_ref[...], preferred_element_type=F32)
    x1_ref[...] = x1
    ms = jnp.mean(x1 * x1, axis=-1, keepdims=True)
    h2 = x1 * lax.rsqrt(ms + EPS) * gffn_ref[...]
    nblk = d // LANES
    for cb in range(nblk):
        hr_ref[pl.ds(cb, tm, stride=nblk), :] = h2[:, cb * LANES:(cb + 1) * LANES]

    h_hi = h2.astype(BF16)
    h_lo = (h2 - h_hi.astype(F32)).astype(BF16)
    wt = wrt_ref[...]
    wt_hi = wt.astype(BF16)
    wt_lo = (wt - wt_hi.astype(F32)).astype(BF16)
    nt = (((1,), (1,)), ((), ()))
    l = (lax.dot_general(wt_hi, h_hi, nt, preferred_element_type=F32)
         + lax.dot_general(wt_lo, h_hi, nt, preferred_element_type=F32)
         + lax.dot_general(wt_hi, h_lo, nt, preferred_element_type=F32)) + brt_ref[...]

    iota = lax.broadcasted_iota(jnp.int32, (n_exp, tm), 0)
    vals, idxs, sels = [], [], []
    for _ in range(TOP_K):
        mk = jnp.max(l, axis=0, keepdims=True)
        ik = jnp.min(jnp.where(l == mk, iota, n_exp), axis=0, keepdims=True)
        sel = iota == ik
        l = jnp.where(sel, -jnp.inf, l)
        vals.append(mk)
        idxs.append(ik)
        sels.append(sel)
    exs = [jnp.exp(v - vals[0]) for v in vals]
    den = exs[0] + exs[1] + exs[2] + exs[3]

    @pl.when(i % tiles_per_chunk == 0)
    def _():
        carry_ref[...] = jnp.zeros_like(carry_ref)

    member = (sels[0] | sels[1] | sels[2] | sels[3])
    mf = jnp.where(member, 1.0, 0.0).astype(F32)
    r_i = lax.broadcasted_iota(jnp.int32, (tm, tm), 0)
    c_i = lax.broadcasted_iota(jnp.int32, (tm, tm), 1)
    earlier = jnp.where(r_i < c_i, 1.0, 0.0).astype(BF16)
    before = jnp.dot(mf.astype(BF16), earlier, preferred_element_type=F32) + carry_ref[...]
    carry_new = carry_ref[...] + jnp.sum(mf, axis=1, keepdims=True)
    carry_ref[...] = carry_new
    cnt_ref[...] = carry_new.astype(jnp.int32)

    ranks = [jnp.sum(jnp.where(sels[k], before, 0.0), axis=0, keepdims=True).astype(jnp.int32)
             for k in range(TOP_K)]
    e_ref[...] = jnp.concatenate(idxs, axis=0)
    r_ref[...] = jnp.concatenate(ranks, axis=0)
    rw_ref[...] = jnp.concatenate([exs[k] / den for k in range(TOP_K)], axis=0)


def _merge(x2, act, o, gates, w_conv_out, b_conv_out, w_attn_out, b_attn_out, w_out, g_ffn, w_router, b_router,
           *, tm, chunk):
    n, d = x2.shape
    n_exp = w_router.shape[1]
    tiles_per_chunk = chunk // tm
    n_chunks = n // chunk
    nblk = d // LANES
    kern = functools.partial(_merge_kernel, tm=tm, d=d, n_exp=n_exp, tiles_per_chunk=tiles_per_chunk)
    row = lambda i: (i, 0)
    return pl.pallas_call(
        kern,
        grid=(n // tm,),
        in_specs=[
            pl.BlockSpec((tm, d), row),
            pl.BlockSpec((tm, d), row),
            pl.BlockSpec((tm, d), row),
            pl.BlockSpec((tm, 2 * d), row),
            _const_spec((d, d)), _const_spec((1, d)),
            _const_spec((d, d)), _const_spec((1, d)),
            _const_spec((d, d)),
            _const_spec((1, d)),
            _const_spec((n_exp, d)), _const_spec((n_exp, 1)),
        ],
        out_specs=[
            pl.BlockSpec((tm, d), row),
            pl.BlockSpec((tm * nblk, LANES), row),
            pl.BlockSpec((TOP_K, tm), lambda i: (0, i)),
            pl.BlockSpec((TOP_K, tm), lambda i: (0, i)),
            pl.BlockSpec((TOP_K, tm), lambda i: (0, i)),
            pl.BlockSpec((None, n_exp, 1), lambda i: (i // tiles_per_chunk, 0, 0)),
        ],
        out_shape=[
            jax.ShapeDtypeStruct((n, d), F32),
            jax.ShapeDtypeStruct((n * nblk, LANES), F32),
            jax.ShapeDtypeStruct((TOP_K, n), jnp.int32),
            jax.ShapeDtypeStruct((TOP_K, n), jnp.int32),
            jax.ShapeDtypeStruct((TOP_K, n), F32),
            jax.ShapeDtypeStruct((n_chunks, n_exp, 1), jnp.int32),
        ],
        scratch_shapes=[pltpu.VMEM((n_exp, 1), F32)],
        compiler_params=pltpu.CompilerParams(dimension_semantics=("arbitrary",), vmem_limit_bytes=VMEM_LIMIT),
        name="merge",
    )(x2, act, o, gates, w_conv_out.astype(BF16), b_conv_out.reshape(1, d), w_attn_out.astype(BF16),
      b_attn_out.reshape(1, d), w_out.astype(BF16), g_ffn.reshape(1, d), w_router.T, b_router.reshape(n_exp, 1))


def _moe_kernel(cnt_ref, pos_ref, w_ref, h_ref, wg_ref, bg_ref, wu_ref, bu_ref, wd_ref, bd_ref, x1_ref, gf_ref,
                out_ref, xs_ref, off_ref, ys_ref, *, chunk, tsub, d, n_exp, bm):
    c = pl.program_id(0)
    s = pl.program_id(1)
    nblk = d // LANES
    n_sub = chunk // tsub

    @pl.when((c == 0) & (s == 0))
    def _():
        xs_ref[...] = jnp.zeros_like(xs_ref)

    @pl.when(s == 0)
    def _():
        def offs(j, acc):
            off_ref[j] = acc
            return acc + cnt_ref[c, j]

        lax.fori_loop(0, n_exp, offs, 0)

    @pl.when(s < n_sub)
    def _():
        t0 = s * tsub

        def dispatch(i, carry):
            row = h_ref[pl.ds(pl.multiple_of(i * nblk, nblk), nblk), :]
            for k in range(TOP_K):
                row0 = pos_ref[k * chunk + t0 + i]
                xs_ref[pl.ds(pl.multiple_of(row0, nblk), nblk), :] = row
            return carry

        lax.fori_loop(0, tsub, dispatch, 0, unroll=8)

    @pl.when((s >= n_sub) & (s < n_sub + n_exp))
    def _():
        e = s - n_sub
        cnt = cnt_ref[c, e]
        base = off_ref[e]

        def ffn(slot0, m, n_valid=None):
            row0 = slot0 * nblk
            x = jnp.concatenate([xs_ref[pl.ds(row0 + cb, m, stride=nblk), :].astype(BF16) for cb in range(nblk)],
                                axis=1)
            g = jnp.dot(x, wg_ref[...], preferred_element_type=F32) + bg_ref[...]
            u = jnp.dot(x, wu_ref[...], preferred_element_type=F32) + bu_ref[...]
            g = jnp.minimum(g, SWIGLU_LIMIT)
            u = jnp.clip(u, -SWIGLU_LIMIT, SWIGLU_LIMIT)
            a = (u + 1.0) * (g * jax.nn.sigmoid(SWIGLU_ALPHA * g))
            y = jnp.dot(a.astype(BF16), wd_ref[...], preferred_element_type=F32) + bd_ref[...]
            if n_valid is not None:
                keep = lax.broadcasted_iota(jnp.int32, (m, LANES), 0) < n_valid
            for cb in range(nblk):
                dst = pl.ds(row0 + cb, m, stride=nblk)
                ycb = y[:, cb * LANES:(cb + 1) * LANES]
                xs_ref[dst, :] = ycb if n_valid is None else jnp.where(keep, ycb, xs_ref[dst, :])

        n_big = jnp.maximum((cnt - (bm + 1)) // (2 * bm), 0)

        def big(b, carry):
            ffn(base + b * 2 * bm, 2 * bm)
            return carry

        lax.fori_loop(0, n_big, big, 0)
        rem = cnt - n_big * 2 * bm
        start = base + n_big * 2 * bm
        mid = jnp.where(rem > 2 * bm, bm, 0)

        @pl.when(rem > 2 * bm)
        def _():
            ffn(start, bm)

        rem = rem - mid
        start = start + mid
        for m in range(bm, 2 * bm + 1, BLOCK_STEP):
            lo = 0 if m == bm else m - BLOCK_STEP

            @pl.when((rem > lo) & (rem <= m))
            def _(m=m):
                ffn(start, m, rem)

    @pl.when(s >= n_sub + n_exp)
    def _():
        t0 = (s - n_sub - n_exp) * tsub

        def combine(i, carry):
            acc = jnp.zeros((nblk, LANES), F32)
            for k in range(TOP_K):
                a = k * chunk + t0 + i
                slot = pl.ds(pl.multiple_of(pos_ref[a], nblk), nblk)
                acc = acc + w_ref[a] * xs_ref[slot, :]
            ys_ref[pl.ds(pl.multiple_of(i * nblk, nblk), nblk), :] = acc
            return carry

        lax.fori_loop(0, tsub, combine, 0, unroll=8)
        y = jnp.concatenate([ys_ref[pl.ds(cb, tsub, stride=nblk), :] for cb in range(nblk)], axis=1)
        x = x1_ref[...] + y
        ms = jnp.mean(x * x, axis=-1, keepdims=True)
        out_ref[...] = x * lax.rsqrt(ms + EPS) * gf_ref[...]


def _moe(counts, pos, rw, hr, wg, bg, wu, bu, wd, bd, x1, g_final, *, chunk, tsub, d, bm):
    n_chunks, n_exp = counts.shape
    nblk = d // LANES
    f = wg.shape[2]
    n_assign = chunk * TOP_K
    n_sub = chunk // tsub
    kern = functools.partial(_moe_kernel, chunk=chunk, tsub=tsub, d=d, n_exp=n_exp, bm=bm)
    smem_blk = lambda: pl.BlockSpec((n_assign,), lambda c, s, cnt: (c,), memory_space=pltpu.SMEM)
    wspec = lambda shp: pl.BlockSpec((None,) + shp, lambda c, s, cnt: (jnp.clip(s - n_sub, 0, n_exp - 1), 0, 0))
    out_tile = lambda c, s, cnt: (c * n_sub + jnp.clip(s - n_sub - n_exp, 0, n_sub - 1), 0)
    grid_spec = pltpu.PrefetchScalarGridSpec(
        num_scalar_prefetch=1,
        grid=(n_chunks, 2 * n_sub + n_exp),
        in_specs=[
            smem_blk(), smem_blk(),
            pl.BlockSpec((tsub * nblk, LANES), lambda c, s, cnt: (c * n_sub + jnp.minimum(s, n_sub - 1), 0)),
            wspec((d, f)), wspec((1, f)), wspec((d, f)), wspec((1, f)), wspec((f, d)), wspec((1, d)),
            pl.BlockSpec((tsub, d), out_tile),
            pl.BlockSpec((1, d), lambda c, s, cnt: (0, 0)),
        ],
        out_specs=pl.BlockSpec((tsub, d), out_tile),
        scratch_shapes=[
            pltpu.VMEM(((n_assign + 2 * bm) * nblk, LANES), F32),
            pltpu.SMEM((n_exp,), jnp.int32),
            pltpu.VMEM((tsub * nblk, LANES), F32),
        ],
    )
    return pl.pallas_call(
        kern,
        grid_spec=grid_spec,
        out_shape=jax.ShapeDtypeStruct((n_chunks * chunk, d), F32),
        compiler_params=pltpu.CompilerParams(dimension_semantics=("arbitrary", "arbitrary"),
                                             vmem_limit_bytes=VMEM_LIMIT_MOE),
        name="moe",
    )(counts, pos, rw, hr, wg, bg, wu, bu, wd, bd, x1, g_final.reshape(1, d))


def _pos_kernel(cnt_ref, e_ref, r_ref, w_ref, pos_ref, wflat_ref, *, n_exp, chunk, nblk):
    cnt = jnp.broadcast_to(cnt_ref[...], (n_exp, LANES))
    hi = (cnt // 256).astype(F32).astype(BF16)
    lo = (cnt % 256).astype(F32).astype(BF16)
    r_i = lax.broadcasted_iota(jnp.int32, (n_exp, n_exp), 0)
    c_i = lax.broadcasted_iota(jnp.int32, (n_exp, n_exp), 1)
    tri = jnp.where(c_i < r_i, 1.0, 0.0).astype(BF16)
    off = (256.0 * jnp.dot(tri, hi, preferred_element_type=F32)
           + jnp.dot(tri, lo, preferred_element_type=F32)).astype(jnp.int32)[:, 0:1]
    e = e_ref[...]
    pos = r_ref[...]
    for j in range(n_exp):
        pos = pos + jnp.where(e == j, off[j:j + 1, :], 0)
    for k in range(TOP_K):
        pos_ref[:, k * chunk:(k + 1) * chunk] = pos[k:k + 1, :] * nblk
        wflat_ref[:, k * chunk:(k + 1) * chunk] = w_ref[k:k + 1, :]


def _pos(counts, e, r, rw, *, chunk, nblk):
    n_chunks, n_exp, _ = counts.shape
    assert chunk < 256 * 256 and chunk % LANES == 0
    blk = pl.BlockSpec((TOP_K, chunk), lambda c: (0, c))
    flat = pl.BlockSpec((None, 1, TOP_K * chunk), lambda c: (c, 0, 0))
    pos, wflat = pl.pallas_call(
        functools.partial(_pos_kernel, n_exp=n_exp, chunk=chunk, nblk=nblk),
        grid=(n_chunks,),
        in_specs=[pl.BlockSpec((None, n_exp, 1), lambda c: (c, 0, 0)), blk, blk, blk],
        out_specs=[flat, flat],
        out_shape=[jax.ShapeDtypeStruct((n_chunks, 1, TOP_K * chunk), jnp.int32),
                   jax.ShapeDtypeStruct((n_chunks, 1, TOP_K * chunk), F32)],
        compiler_params=pltpu.CompilerParams(dimension_semantics=("arbitrary",), vmem_limit_bytes=VMEM_LIMIT),
        name="pos",
    )(counts, e, r, rw)
    return pos.reshape(-1), wflat.reshape(-1)


def _block_sizes(batch, seq):
    n = batch * seq
    tm = min(512, n)
    tt = min(512, seq)
    chunk = min(2048, n)
    return tm, tt, chunk


def kernel(x, g_mix, w_in, b_in, conv_w, conv_b, conv_ln_g, conv_ln_b, w_conv_out, b_conv_out, attn_sinks,
           w_attn_out, b_attn_out, w_out, g_ffn, w_router, b_router, w_gate, b_gate, w_up, b_up, w_down,
           b_down, g_final):
    batch, seq, d = x.shape
    n = batch * seq
    c_conv = conv_w.shape[1]
    n_heads = attn_sinks.shape[0]
    n_q = n_heads * HEAD_DIM
    n_kv = (n_heads // KV_GROUP) * HEAD_DIM
    n_exp = w_router.shape[1]
    assert n_kv == LANES and seq % WINDOW == 0 and d % LANES == 0
    tm, tt, chunk = _block_sizes(batch, seq)
    assert n % tm == 0 and seq % tt == 0 and n % chunk == 0 and chunk % tm == 0

    x2 = x.reshape(n, d)
    u, q, kv, gates, wgb, wub, wdb = _inproj(x2, g_mix, w_in, b_in, w_gate, w_up, w_down,
                                             c_conv=c_conv, n_q=n_q, n_kv=n_kv, tm=tm)
    act = _conv(u.reshape(batch, seq, c_conv), conv_w, conv_b, conv_ln_g, conv_ln_b, tt=tt)
    o = _attn(q.reshape(batch, seq, n_q), kv.reshape(batch, seq, 2 * n_kv), attn_sinks, n_heads,
              nsub=max(ns for ns in (1, 2, 4) if seq % (ns * WINDOW) == 0))
    x1, hr, e, r, rw, counts = _merge(
        x2, act.reshape(n, c_conv), o.reshape(n, n_q), gates, w_conv_out, b_conv_out, w_attn_out, b_attn_out,
        w_out, g_ffn, w_router, b_router, tm=tm, chunk=chunk)
    pos, wflat = _pos(counts, e, r, rw, chunk=chunk, nblk=d // LANES)
    out = _moe(counts.reshape(n // chunk, n_exp), pos, wflat, hr,
               wgb, b_gate.reshape(n_exp, 1, -1), wub, b_up.reshape(n_exp, 1, -1),
               wdb, b_down.reshape(n_exp, 1, -1), x1, g_final, chunk=chunk, tsub=tm, d=d, bm=128)
    return out.reshape(batch, seq, d)
```
